```python
import math
import jax, jax.numpy as jnp
from jax import lax
import numpy as np

D_MODEL = 1024
BATCH = 2
SEQ = 8192
DEPTH = 4
DEC_BATCH = 128
DEC_SEQ = 8
PAST_LEN = 2048
PAGE_SIZE = 128

N_HEADS = 8
HEAD_DIM = D_MODEL // (2 * N_HEADS)
V_DIM = 2 * HEAD_DIM
ATTN_WIDTH = N_HEADS * V_DIM
QK_WIDTH = N_HEADS * 2 * HEAD_DIM
N_BUCKETS = 32
MAX_DISTANCE = 128
Q_BLOCK = 128
POOL_WIDTH = D_MODEL
POOL_WINDOWS = (2, 4, 8, 16)
POOL_GROUP = POOL_WIDTH // len(POOL_WINDOWS)
POOL_BUF = max(POOL_WINDOWS) - 1
LRU_WIDTH = D_MODEL
LRU_BLOCKS = 16
LRU_BLOCK = LRU_WIDTH // LRU_BLOCKS
CONV_WIDTH = 4
LRU_C = 8.0
N_BRANCH = 3
N_IN = 2 * QK_WIDTH + 2 * (ATTN_WIDTH + POOL_WIDTH + LRU_WIDTH) + N_BRANCH * D_MODEL
RMS_EPS = 1e-6
NEG_INF = -1e30

kernel_name = 'hybrid_diffattn_pool_rglru_decode_step'


def _split_points():
    widths = (QK_WIDTH, QK_WIDTH, ATTN_WIDTH, POOL_WIDTH, LRU_WIDTH, ATTN_WIDTH, POOL_WIDTH, LRU_WIDTH)
    pts, acc = [], 0
    for w in widths:
        acc += w
        pts.append(acc)
    return pts


def rms_norm(x, g):
    x32 = x.astype(jnp.float32)
    y = x32 * lax.rsqrt(jnp.mean(x32 * x32, axis=-1, keepdims=True) + RMS_EPS)
    return y.astype(x.dtype) * g


def t5_bucket(q_pos, k_pos):
    n = jnp.maximum(q_pos[:, None] - k_pos[None, :], 0)
    max_exact = N_BUCKETS // 2
    nf = jnp.maximum(n, 1).astype(jnp.float32)
    large = max_exact + (jnp.log(nf / max_exact) / math.log(MAX_DISTANCE / max_exact)
                         * (N_BUCKETS - max_exact)).astype(jnp.int32)
    large = jnp.minimum(large, N_BUCKETS - 1)
    return jnp.where(n < max_exact, n, large)


def diff_attn_block(q, q_pos, k_segs, v_segs, k_pos_segs, lam, rel_bias):
    k_pos = jnp.concatenate(k_pos_segs)
    logits = jnp.concatenate(
        [jnp.einsum('bqhcd,bkhcd->bhcqk', q, kk).astype(jnp.float32) for kk in k_segs], axis=-1)
    bias = jnp.transpose(rel_bias[t5_bucket(q_pos, k_pos)], (2, 0, 1)).astype(jnp.float32)
    mask = k_pos[None, :] <= q_pos[:, None]
    logits = jnp.where(mask, logits + bias[None, :, None], NEG_INF)
    p = jax.nn.softmax(logits, axis=-1)
    a = (p[:, :, 0] - lam * p[:, :, 1]).astype(v_segs[0].dtype)
    out, start = None, 0
    for vv in v_segs:
        n = vv.shape[1]
        part = jnp.einsum('bhqk,bkhe->bqhe', a[..., start:start + n], vv)
        out = part if out is None else out + part
        start += n
    return out


def prompt_attention(q, k, v, lam, rel_bias):
    b, s = q.shape[:2]
    nb = s // Q_BLOCK
    pos = jnp.arange(s, dtype=jnp.int32)
    qb = jnp.moveaxis(q.reshape(b, nb, Q_BLOCK, N_HEADS, 2, HEAD_DIM), 1, 0)
    pb = pos.reshape(nb, Q_BLOCK)
    ob = lax.map(lambda a: diff_attn_block(a[0], a[1], (k,), (v,), (pos,), lam, rel_bias), (qb, pb))
    return jnp.moveaxis(ob, 0, 1).reshape(b, s, N_HEADS, V_DIM)


def pool_mix(u, buf, pos0, pool_w, pool_scale):
    b, t = u.shape[:2]
    up = jnp.concatenate([buf, u], axis=1).astype(jnp.float32)
    cs = jnp.concatenate([jnp.zeros_like(up[:, :1]), jnp.cumsum(up, axis=1)], axis=1)
    end = cs[:, POOL_BUF + 1:]
    tpos = jnp.arange(t, dtype=jnp.int32)
    means = []
    for gi, win in enumerate(POOL_WINDOWS):
        lo, hi = gi * POOL_GROUP, (gi + 1) * POOL_GROUP
        start = cs[:, POOL_BUF + 1 - win:POOL_BUF + 1 - win + t, lo:hi]
        cnt = jnp.minimum(pos0 + tpos + 1, win).astype(jnp.float32)[None, :, None]
        means.append((end[..., lo:hi] - start) / cnt)
    pooled = jnp.concatenate(means, axis=-1).astype(u.dtype) - u
    mixed = jnp.einsum('btgc,gcd->btgd', pooled.reshape(b, t, len(POOL_WINDOWS), POOL_GROUP), pool_w)
    return mixed.reshape(b, t, POOL_WIDTH) * pool_scale


def _lin_combine(c1, c2):
    a1, b1 = c1
    a2, b2 = c2
    return a1 * a2, a2 * b1 + b2


def rglru(u, conv_buf, h0, conv_w, conv_b, ga_w, ga_b, gx_w, gx_b, lru_lambda):
    b, t = u.shape[:2]
    up = jnp.concatenate([conv_buf, u], axis=1)
    xc = conv_b + up[:, 0:t] * conv_w[0]
    for j in range(1, CONV_WIDTH):
        xc = xc + up[:, j:j + t] * conv_w[j]
    xb = xc.reshape(b, t, LRU_BLOCKS, LRU_BLOCK)
    r = jax.nn.sigmoid(jnp.einsum('btnc,ncd->btnd', xb, ga_w) + ga_b).reshape(b, t, LRU_WIDTH)
    i = jax.nn.sigmoid(jnp.einsum('btnc,ncd->btnd', xb, gx_w) + gx_b).reshape(b, t, LRU_WIDTH)
    log_a = -LRU_C * r.astype(jnp.float32) * jax.nn.softplus(-lru_lambda.astype(jnp.float32))
    a = jnp.exp(log_a)
    bt = jnp.sqrt(-jnp.expm1(2.0 * log_a)) * (i * xc).astype(jnp.float32)
    bt = bt.at[:, 0].add(a[:, 0] * h0.astype(jnp.float32))
    _, h = lax.associative_scan(_lin_combine, (a, bt), axis=1)
    return h.astype(u.dtype), up[:, t:], h[:, -1].astype(h0.dtype)


def mixer_layer(x, pos0, past_kv, pool_buf, conv_buf, h0, lam_init, rel_bias,
                norm_g, w_in, q_norm_g, k_norm_g, lam_q1, lam_k1, lam_q2, lam_k2, subln_g,
                pool_w, pool_scale, conv_w, conv_b, ga_w, ga_b, gx_w, gx_b, lru_lambda, w_out):
    b, t, _ = x.shape
    xn = rms_norm(x, norm_g)
    proj = xn @ w_in
    q, k, v, u_pool, u_lru, z_attn, z_pool, z_lru, g = jnp.split(proj, _split_points(), axis=-1)
    q = rms_norm(q.reshape(b, t, N_HEADS, 2, HEAD_DIM), q_norm_g) * (HEAD_DIM ** -0.5)
    k = rms_norm(k.reshape(b, t, N_HEADS, 2, HEAD_DIM), k_norm_g)
    v = v.reshape(b, t, N_HEADS, V_DIM)
    lam = (jnp.exp(jnp.sum(lam_q1.astype(jnp.float32) * lam_k1.astype(jnp.float32)))
           - jnp.exp(jnp.sum(lam_q2.astype(jnp.float32) * lam_k2.astype(jnp.float32))) + lam_init)
    q_pos = pos0 + jnp.arange(t, dtype=jnp.int32)
    if past_kv is None:
        o = prompt_attention(q, k, v, lam, rel_bias)
    else:
        pk, pv = past_kv
        p_len = pk.shape[1]
        o = diff_attn_block(q, q_pos, (pk.reshape(b, p_len, N_HEADS, 2, HEAD_DIM), k), (pv, v),
                            (jnp.arange(p_len, dtype=jnp.int32), q_pos), lam, rel_bias)
    o_attn = (rms_norm(o, subln_g) * (1.0 - lam_init)).reshape(b, t, ATTN_WIDTH)
    o_pool = pool_mix(u_pool, pool_buf, pos0, pool_w, pool_scale)
    o_lru, conv_new, h_new = rglru(u_lru, conv_buf, h0, conv_w, conv_b, ga_w, ga_b, gx_w, gx_b, lru_lambda)
    gates = jax.nn.sigmoid(g.reshape(b, t, N_BRANCH, D_MODEL))
    m = (gates[:, :, 0] * (o_attn * jax.nn.silu(z_attn))
         + gates[:, :, 1] * (o_pool * jax.nn.silu(z_pool))
         + gates[:, :, 2] * (o_lru * jax.nn.silu(z_lru)))
    y = x + m @ w_out
    pool_new = jnp.concatenate([pool_buf, u_pool], axis=1)[:, -POOL_BUF:]
    return y, k.reshape(b, t, N_HEADS, 2 * HEAD_DIM), v, pool_new, conv_new, h_new


def setup_inputs(seed: int = 0) -> dict:
    key = jax.random.key(seed)
    ks = jax.random.split(key, 32)
    f32 = jnp.float32
    n_pages = PAST_LEN // PAGE_SIZE
    n_used = DEC_BATCH * n_pages
    n_phys = n_used + max(1, n_used // 4)

    def nrm(k, shape, s):
        return s * jax.random.normal(k, shape, f32)

    def gain(k, shape):
        return 1.0 + 0.02 * jax.random.normal(k, shape, f32)

    page_table = jax.random.permutation(ks[0], n_phys)[:n_used].reshape(DEC_BATCH, n_pages).astype(jnp.int32)
    lam_sig = jax.random.uniform(ks[1], (DEPTH, LRU_WIDTH), f32, 0.9, 0.999)
    lru_lambda = jnp.log(lam_sig) - jnp.log1p(-lam_sig)
    return {
        'x_prompt': nrm(ks[2], (BATCH, SEQ, D_MODEL), 1.0),
        'x_sample': nrm(ks[3], (DEC_BATCH, DEC_SEQ, D_MODEL), 1.0),
        'cache_k': nrm(ks[4], (DEPTH, n_phys, PAGE_SIZE, N_HEADS, 2 * HEAD_DIM), 1.0),
        'cache_v': nrm(ks[5], (DEPTH, n_phys, PAGE_SIZE, N_HEADS, V_DIM), 1.0),
        'page_table': page_table,
        'state_pool': nrm(ks[6], (DEPTH, DEC_BATCH, POOL_BUF, POOL_WIDTH), 1.0),
        'state_conv': nrm(ks[7], (DEPTH, DEC_BATCH, CONV_WIDTH - 1, LRU_WIDTH), 1.0),
        'state_h': nrm(ks[8], (DEPTH, DEC_BATCH, LRU_WIDTH), 0.5),
        'rel_bias': nrm(ks[9], (N_BUCKETS, N_HEADS), 0.5),
        'norm_g': gain(ks[10], (DEPTH, D_MODEL)),
        'w_in': nrm(ks[11], (DEPTH, D_MODEL, N_IN), D_MODEL ** -0.5),
        'q_norm_g': gain(ks[12], (DEPTH, HEAD_DIM)),
        'k_norm_g': gain(ks[13], (DEPTH, HEAD_DIM)),
        'lam_q1': nrm(ks[14], (DEPTH, HEAD_DIM), 0.1),
        'lam_k1': nrm(ks[15], (DEPTH, HEAD_DIM), 0.1),
        'lam_q2': nrm(ks[16], (DEPTH, HEAD_DIM), 0.1),
        'lam_k2': nrm(ks[17], (DEPTH, HEAD_DIM), 0.1),
        'subln_g': gain(ks[18], (DEPTH, V_DIM)),
        'pool_w': nrm(ks[19], (DEPTH, len(POOL_WINDOWS), POOL_GROUP, POOL_GROUP), POOL_GROUP ** -0.5),
        'pool_scale': gain(ks[20], (DEPTH, POOL_WIDTH)),
        'conv_w': nrm(ks[21], (DEPTH, CONV_WIDTH, LRU_WIDTH), CONV_WIDTH ** -0.5),
        'conv_b': nrm(ks[22], (DEPTH, LRU_WIDTH), 0.01),
        'gate_a_w': nrm(ks[23], (DEPTH, LRU_BLOCKS, LRU_BLOCK, LRU_BLOCK), LRU_BLOCK ** -0.5),
        'gate_a_b': nrm(ks[24], (DEPTH, LRU_BLOCKS, LRU_BLOCK), 0.01),
        'gate_x_w': nrm(ks[25], (DEPTH, LRU_BLOCKS, LRU_BLOCK, LRU_BLOCK), LRU_BLOCK ** -0.5),
        'gate_x_b': nrm(ks[26], (DEPTH, LRU_BLOCKS, LRU_BLOCK), 0.01),
        'lru_lambda': lru_lambda,
        'w_out': nrm(ks[27], (DEPTH, D_MODEL, D_MODEL), D_MODEL ** -0.5),
    }


def reference(x_prompt, x_sample, cache_k, cache_v, page_table, state_pool, state_conv, state_h,
              rel_bias, norm_g, w_in, q_norm_g, k_norm_g, lam_q1, lam_k1, lam_q2, lam_k2, subln_g,
              pool_w, pool_scale, conv_w, conv_b, gate_a_w, gate_a_b, gate_x_w, gate_x_b,
              lru_lambda, w_out):
    b = x_prompt.shape[0]
    db, n_pages = page_table.shape
    past_len = n_pages * PAGE_SIZE
    dt = x_prompt.dtype
    pool0 = jnp.zeros((b, POOL_BUF, POOL_WIDTH), dt)
    conv0 = jnp.zeros((b, CONV_WIDTH - 1, LRU_WIDTH), dt)
    h00 = jnp.zeros((b, LRU_WIDTH), dt)
    yp, ys = x_prompt, x_sample
    kp_l, vp_l, ks_l, vs_l, pp_l, ps_l, cp_l, cs_l, hp_l, hs_l = ([] for _ in range(10))
    for l in range(DEPTH):
        lam_init = 0.8 - 0.6 * math.exp(-0.3 * l)
        lw = (norm_g[l], w_in[l], q_norm_g[l], k_norm_g[l], lam_q1[l], lam_k1[l], lam_q2[l], lam_k2[l],
              subln_g[l], pool_w[l], pool_scale[l], conv_w[l], conv_b[l], gate_a_w[l], gate_a_b[l],
              gate_x_w[l], gate_x_b[l], lru_lambda[l], w_out[l])
        yp, kp, vp, pp, cp, hp = mixer_layer(yp, 0, None, pool0, conv0, h00, lam_init, rel_bias, *lw)
        pk = cache_k[l, page_table].reshape(db, past_len, N_HEADS, 2 * HEAD_DIM)
        pv = cache_v[l, page_table].reshape(db, past_len, N_HEADS, V_DIM)
        ys, ksm, vsm, psm, csm, hsm = mixer_layer(ys, past_len, (pk, pv), state_pool[l], state_conv[l],
                                                  state_h[l], lam_init, rel_bias, *lw)
        kp_l.append(kp); vp_l.append(vp); ks_l.append(ksm); vs_l.append(vsm)
        pp_l.append(pp); ps_l.append(psm); cp_l.append(cp); cs_l.append(csm)
        hp_l.append(hp); hs_l.append(hsm)
    return (yp, ys, jnp.stack(kp_l), jnp.stack(vp_l), jnp.stack(ks_l), jnp.stack(vs_l),
            jnp.stack(pp_l), jnp.stack(ps_l), jnp.stack(cp_l), jnp.stack(cs_l),
            jnp.stack(hp_l), jnp.stack(hs_l))
```

```python
import functools
import math

import numpy as np
import jax
import jax.numpy as jnp
from jax import lax
from jax.experimental import pallas as pl
from jax.experimental.pallas import tpu as pltpu

F32 = jnp.float32
BF16 = jnp.bfloat16

RMS_EPS = 1e-6
NEG_INF = -1e30
LOG2E = 1.4426950408889634

N_HEADS = 8
HEAD_DIM = 64
V_DIM = 2 * HEAD_DIM
N_BUCKETS = 32
MAX_DISTANCE = 128
POOL_WINDOWS = (2, 4, 8, 16)
POOL_BUF = max(POOL_WINDOWS) - 1
POOL_HIST = 16
CONV_WIDTH = 4
CONV_HIST = 8
LRU_BLOCKS = 16
LRU_C = 8.0
N_BRANCH = 3
GATE_TILE = 256

ROW_TILE = 1024
ATTN_TILE = 512
SEQ_TILE = 256
VMEM_LIMIT = 56 * 1024 * 1024


def _cparams(*sem):
    return pltpu.CompilerParams(dimension_semantics=sem, vmem_limit_bytes=VMEM_LIMIT)


def _tile(n, pref):
    return pref if n % pref == 0 else n


def _rms_rows(x, g):
    ms = jnp.mean(x * x, axis=-1, keepdims=True)
    return x * lax.rsqrt(ms + RMS_EPS) * g


def _norm_kernel(x_ref, g_ref, o_ref):
    o_ref[...] = _rms_rows(x_ref[...], g_ref[...]).astype(BF16)


def _norm_call(x, g):
    r, d = x.shape
    tm = _tile(r, ROW_TILE)
    return pl.pallas_call(
        _norm_kernel,
        grid=(r // tm,),
        in_specs=[pl.BlockSpec((tm, d), lambda i: (i, 0)), pl.BlockSpec((1, d), lambda i: (0, 0))],
        out_specs=pl.BlockSpec((tm, d), lambda i: (i, 0)),
        out_shape=jax.ShapeDtypeStruct((r, d), BF16),
        compiler_params=_cparams("parallel"),
        name="prenorm",
    )(x, g)


def _head_norm(acc, pmat_ref, g_ref):
    ss = jnp.dot((acc * acc).astype(BF16), pmat_ref[...], preferred_element_type=F32)
    return acc * lax.rsqrt(ss * (1.0 / HEAD_DIM) + RMS_EPS) * g_ref[...]


def _proj_q_kernel(xn_ref, w_ref, pmat_ref, g_ref, q_ref):
    acc = jnp.dot(xn_ref[...], w_ref[...], preferred_element_type=F32)
    q_ref[...] = _head_norm(acc, pmat_ref, g_ref).astype(BF16)


def _proj_k_kernel(xn_ref, w_ref, pmat_ref, g_ref, k_ref, kb_ref):
    acc = jnp.dot(xn_ref[...], w_ref[...], preferred_element_type=F32)
    kn = _head_norm(acc, pmat_ref, g_ref)
    k_ref[...] = kn
    kb_ref[...] = kn.astype(BF16)


def _proj_v_kernel(xn_ref, w_ref, v_ref, vb_ref):
    acc = jnp.dot(xn_ref[...], w_ref[...], preferred_element_type=F32)
    v_ref[...] = acc
    vb_ref[...] = acc.astype(BF16)


def _proj_u_kernel(xn_ref, w_ref, u_ref):
    u_ref[...] = jnp.dot(xn_ref[...], w_ref[...], preferred_element_type=F32)


def _proj_gz_kernel(xn_ref, wz_ref, wg_ref, o_ref):
    xn = xn_ref[...]
    z = jnp.dot(xn, wz_ref[...], preferred_element_type=F32)
    g = jnp.dot(xn, wg_ref[...], preferred_element_type=F32)
    o_ref[...] = jax.nn.sigmoid(g) * (z * jax.nn.sigmoid(z))


def _proj_calls(xn, w_bf, layer, pmat, gq, gk):
    r, d = xn.shape
    tm = _tile(r, ROW_TILE)
    nr = r // tm
    x_spec = pl.BlockSpec((tm, d), lambda i: (i, 0))
    full = lambda a: pl.BlockSpec(a.shape, lambda i: (0,) * a.ndim)
    o_spec = pl.BlockSpec((tm, d), lambda i: (i, 0))

    def w_spec(j):
        return pl.BlockSpec((None, d, d), lambda i: (layer, 0, j))

    qn = pl.pallas_call(
        _proj_q_kernel, grid=(nr,),
        in_specs=[x_spec, w_spec(0), full(pmat), full(gq)],
        out_specs=o_spec, out_shape=jax.ShapeDtypeStruct((r, d), BF16),
        compiler_params=_cparams("parallel"), name="proj_q")(xn, w_bf, pmat, gq)
    kn, kb = pl.pallas_call(
        _proj_k_kernel, grid=(nr,),
        in_specs=[x_spec, w_spec(1), full(pmat), full(gk)],
        out_specs=[o_spec, o_spec],
        out_shape=[jax.ShapeDtypeStruct((r, d), F32), jax.ShapeDtypeStruct((r, d), BF16)],
        compiler_params=_cparams("parallel"), name="proj_k")(xn, w_bf, pmat, gk)
    v, vb = pl.pallas_call(
        _proj_v_kernel, grid=(nr,),
        in_specs=[x_spec, w_spec(2)],
        out_specs=[o_spec, o_spec],
        out_shape=[jax.ShapeDtypeStruct((r, d), F32), jax.ShapeDtypeStruct((r, d), BF16)],
        compiler_params=_cparams("parallel"), name="proj_v")(xn, w_bf)
    u = pl.pallas_call(
        _proj_u_kernel, grid=(2, nr),
        in_specs=[pl.BlockSpec((tm, d), lambda j, i: (i, 0)),
                  pl.BlockSpec((None, d, d), lambda j, i: (layer, 0, 3 + j))],
        out_specs=pl.BlockSpec((None, tm, d), lambda j, i: (j, i, 0)),
        out_shape=jax.ShapeDtypeStruct((2, r, d), F32),
        compiler_params=_cparams("parallel", "parallel"), name="proj_u")(xn, w_bf)
    gz = pl.pallas_call(
        _proj_gz_kernel, grid=(N_BRANCH, nr),
        in_specs=[pl.BlockSpec((tm, d), lambda j, i: (i, 0)),
                  pl.BlockSpec((None, d, d), lambda j, i: (layer, 0, 5 + j)),
                  pl.BlockSpec((None, d, d), lambda j, i: (layer, 0, 8 + j))],
        out_specs=pl.BlockSpec((None, tm, d), lambda j, i: (j, i, 0)),
        out_shape=jax.ShapeDtypeStruct((N_BRANCH, r, d), F32),
        compiler_params=_cparams("parallel", "parallel"), name="proj_gz")(xn, w_bf, w_bf)
    return qn, kn, kb, v, vb, u, gz


def _lam_value(lamp_ref, lam_init):
    lp = lamp_ref[...]
    s1 = jnp.sum(lp[0:1] * lp[1:2], axis=1, keepdims=True)
    s2 = jnp.sum(lp[2:3] * lp[3:4], axis=1, keepdims=True)
    return jnp.exp(s1) - jnp.exp(s2) + lam_init


def _bias_by_distance(rel_bias, n_max):
    n = jnp.arange(n_max, dtype=jnp.int32)
    max_exact = N_BUCKETS // 2
    nf = jnp.maximum(n, 1).astype(F32)
    large = max_exact + (jnp.log(nf / max_exact) / math.log(MAX_DISTANCE / max_exact)
                         * (N_BUCKETS - max_exact)).astype(jnp.int32)
    large = jnp.minimum(large, N_BUCKETS - 1)
    return rel_bias[jnp.where(n < max_exact, n, large)]


def _far_distance():
    max_exact = N_BUCKETS // 2
    n = np.arange(1, 4 * MAX_DISTANCE, dtype=np.int64)
    large = max_exact + (np.log(n.astype(np.float32) / np.float32(max_exact))
                         / np.float32(math.log(MAX_DISTANCE / max_exact))
                         * (N_BUCKETS - max_exact)).astype(np.int64)
    not_last = np.nonzero((n < max_exact) | (large < N_BUCKETS - 1))[0]
    return int(n[not_last[-1]] + 1) + 2


def _attn_prompt_kernel(qi_ref, ki_ref, qT_ref, k_ref, vT_ref, bias_ref, gz_ref, lamp_ref, sg_ref,
                        o_ref, qp_sc, m_sc, l_sc, acc_sc, *, tq, lam_init):
    t = pl.program_id(2)
    qi = qi_ref[t]
    ki = ki_ref[t]

    @pl.when(ki == 0)
    def _init():
        qT = qT_ref[...]
        row = lax.broadcasted_iota(jnp.int32, qT.shape, 0)
        zero = jnp.zeros_like(qT)
        qp_sc[:, :tq] = jnp.where(row < HEAD_DIM, qT, zero)
        qp_sc[:, tq:] = jnp.where(row >= HEAD_DIM, qT, zero)
        m_sc[...] = jnp.full(m_sc.shape, NEG_INF, F32)
        l_sc[...] = jnp.zeros(l_sc.shape, F32)
        acc_sc[...] = jnp.zeros(acc_sc.shape, F32)

    def step(add_bias):
        s = jnp.dot(k_ref[...], qp_sc[...], preferred_element_type=F32)
        if add_bias:
            b = bias_ref[...]
            s = jnp.concatenate([s[:, :tq] + b, s[:, tq:] + b], axis=1)
        m_old = m_sc[...]
        m_new = jnp.maximum(m_old, jnp.max(s, axis=0, keepdims=True))
        alpha = jnp.exp2(m_old - m_new)
        p = jnp.exp2(s - m_new)
        l_sc[...] = alpha * l_sc[...] + jnp.sum(p, axis=0, keepdims=True)
        acc_sc[...] = alpha * acc_sc[...] + jnp.dot(vT_ref[...], p.astype(BF16),
                                                     preferred_element_type=F32)
        m_sc[...] = m_new

    near = ki + 1 >= qi

    @pl.when(near)
    def _near():
        step(True)

    @pl.when(jnp.logical_not(near))
    def _far():
        step(False)

    @pl.when(ki == qi)
    def _finish():
        lam = _lam_value(lamp_ref, lam_init)
        on = acc_sc[...] * (1.0 / l_sc[...])
        o = on[:, :tq] - lam * on[:, tq:]
        ms = jnp.mean(o * o, axis=0, keepdims=True)
        y = o * lax.rsqrt(ms + RMS_EPS) * sg_ref[...] * (1.0 - lam_init)
        o_ref[...] = y.T * gz_ref[...]


def _prompt_bias_tiles(bvec, t):
    kk = jnp.arange(t, dtype=jnp.int32)[:, None]
    qq = jnp.arange(t, dtype=jnp.int32)[None, :]
    far = bvec[-1]
    rel = (bvec - far[None, :]) * LOG2E
    diag = jnp.where((qq >= kk)[..., None], rel[jnp.maximum(qq - kk, 0)], NEG_INF)
    sub = rel[t + qq - kk]
    return jnp.transpose(jnp.stack([diag, sub]), (3, 0, 1, 2))


def _attn_prompt_call(qn, kb, vb, gz_attn, bias_tiles, lamp, sg_col, lam_init, b, s):
    d = qn.shape[-1]
    t = _tile(s, ATTN_TILE)
    assert t + 1 >= _far_distance(), "far key blocks must all sit in the last relative-bias bucket"
    nq = s // t
    pairs = [(q, k) for q in range(nq) for k in range(q + 1)]
    qi_arr = jnp.asarray(np.array([p[0] for p in pairs], np.int32))
    ki_arr = jnp.asarray(np.array([p[1] for p in pairs], np.int32))
    qT = jnp.transpose(qn.reshape(b, s, N_HEADS, V_DIM), (0, 2, 3, 1))
    vT = jnp.transpose(vb.reshape(b, s, N_HEADS, V_DIM), (0, 2, 3, 1))
    k3 = kb.reshape(b, s, d)
    gz3 = gz_attn.reshape(b, s, d)
    grid_spec = pltpu.PrefetchScalarGridSpec(
        num_scalar_prefetch=2,
        grid=(b, N_HEADS, len(pairs)),
        in_specs=[
            pl.BlockSpec((None, None, V_DIM, t), lambda bi, h, i, qr, kr: (bi, h, 0, qr[i])),
            pl.BlockSpec((None, t, V_DIM), lambda bi, h, i, qr, kr: (bi, kr[i], h)),
            pl.BlockSpec((None, None, V_DIM, t), lambda bi, h, i, qr, kr: (bi, h, 0, kr[i])),
            pl.BlockSpec((None, None, t, t),
                         lambda bi, h, i, qr, kr: (h, jnp.where(kr[i] == qr[i], 0, 1), 0, 0)),
            pl.BlockSpec((None, t, V_DIM), lambda bi, h, i, qr, kr: (bi, qr[i], h)),
            pl.BlockSpec(lamp.shape, lambda bi, h, i, qr, kr: (0, 0)),
            pl.BlockSpec(sg_col.shape, lambda bi, h, i, qr, kr: (0, 0)),
        ],
        out_specs=pl.BlockSpec((None, t, V_DIM), lambda bi, h, i, qr, kr: (bi, qr[i], h)),
        scratch_shapes=[pltpu.VMEM((V_DIM, 2 * t), BF16), pltpu.VMEM((1, 2 * t), F32),
                        pltpu.VMEM((1, 2 * t), F32), pltpu.VMEM((V_DIM, 2 * t), F32)],
    )
    out = pl.pallas_call(
        functools.partial(_attn_prompt_kernel, tq=t, lam_init=lam_init),
        grid_spec=grid_spec,
        out_shape=jax.ShapeDtypeStruct((b, s, d), F32),
        compiler_params=_cparams("parallel", "parallel", "arbitrary"),
        name="attn_prompt",
    )(qi_arr, ki_arr, qT, k3, vT, bias_tiles, gz3, lamp, sg_col)
    return out.reshape(b * s, d)


def _attn_sample_kernel(pt_ref, qbt_ref, kc_ref, vc_ref, kn_ref, vn_ref, bias_ref, gz_ref, lamp_ref,
                        sg_ref, o_ref, m_sc, l_sc, acc_sc, kp_sc, vp_sc, *, n_pages, t_new, lam_init):
    p = pl.program_id(1)

    @pl.when(p == 0)
    def _init():
        m_sc[...] = jnp.full(m_sc.shape, NEG_INF, F32)
        l_sc[...] = jnp.zeros(l_sc.shape, F32)
        acc_sc[...] = jnp.zeros(acc_sc.shape, F32)

    def step(kpage, vpage):
        s = lax.dot_general(qbt_ref[...], kpage.astype(BF16), (((1,), (1,)), ((), ())),
                            preferred_element_type=F32)
        s = s + bias_ref[...]
        m_old = m_sc[...]
        m_new = jnp.maximum(m_old, jnp.max(s, axis=1, keepdims=True))
        alpha = jnp.exp2(m_old - m_new)
        pr = jnp.exp2(s - m_new)
        l_sc[...] = alpha * l_sc[...] + jnp.sum(pr, axis=1, keepdims=True)
        acc_sc[...] = alpha * acc_sc[...] + jnp.dot(pr.astype(BF16), vpage.astype(BF16),
                                                     preferred_element_type=F32)
        m_sc[...] = m_new

    @pl.when(p < n_pages)
    def _cached():
        step(kc_ref[...], vc_ref[...])

    @pl.when(p == n_pages)
    def _new_tokens():
        kp_sc[...] = jnp.zeros(kp_sc.shape, F32)
        vp_sc[...] = jnp.zeros(vp_sc.shape, F32)
        kp_sc[0:t_new, :] = kn_ref[...]
        vp_sc[0:t_new, :] = vn_ref[...]
        step(kp_sc[...], vp_sc[...])
        lam = _lam_value(lamp_ref, lam_init)
        on = acc_sc[...] * (1.0 / l_sc[...])
        for h in range(N_HEADS):
            lanes = slice(h * V_DIM, (h + 1) * V_DIM)
            r1 = (2 * h) * t_new
            r2 = (2 * h + 1) * t_new
            o = on[r1:r1 + t_new, lanes] - lam * on[r2:r2 + t_new, lanes]
            ms = jnp.mean(o * o, axis=1, keepdims=True)
            y = o * lax.rsqrt(ms + RMS_EPS) * sg_ref[...] * (1.0 - lam_init)
            o_ref[:, lanes] = y * gz_ref[:, lanes]


def _sample_bias(bvec, past, page, t_new):
    j = jnp.arange(past + page, dtype=jnp.int32)[None, :]
    tt = jnp.arange(t_new, dtype=jnp.int32)[:, None]
    dist = past + tt - j
    vals = bvec[jnp.clip(dist, 0, bvec.shape[0] - 1)] * LOG2E
    vals = jnp.where((dist >= 0)[..., None], vals, NEG_INF)
    vals = jnp.transpose(vals, (2, 0, 1))
    vals = jnp.broadcast_to(vals[:, None], (N_HEADS, 2, t_new, past + page))
    return vals.reshape(N_HEADS * 2 * t_new, past + page)


def _attn_sample_call(qn, kn, v, gz_attn, cache_k2, cache_v2, page_flat, layer_off, bias, lamp, sg_row,
                      lam_init, db, t_new, n_pages, page):
    d = qn.shape[-1]
    n_grp = 2 * N_HEADS
    rows = n_grp * t_new
    q4 = jnp.transpose(qn.reshape(db, t_new, n_grp, HEAD_DIM), (0, 2, 1, 3))
    eye = jnp.eye(n_grp, dtype=BF16)
    qbt = (q4[:, :, :, None, :] * eye[None, :, None, :, None]).reshape(db, rows, d)
    kn3 = kn.reshape(db, t_new, d)
    v3 = v.reshape(db, t_new, d)
    gz3 = gz_attn.reshape(db, t_new, d)

    def page_map(b, p, pt):
        return (layer_off + pt[b * n_pages + jnp.minimum(p, n_pages - 1)], 0, 0)

    grid_spec = pltpu.PrefetchScalarGridSpec(
        num_scalar_prefetch=1,
        grid=(db, n_pages + 1),
        in_specs=[
            pl.BlockSpec((None, rows, d), lambda b, p, pt: (b, 0, 0)),
            pl.BlockSpec((None, page, d), page_map),
            pl.BlockSpec((None, page, d), page_map),
            pl.BlockSpec((None, t_new, d), lambda b, p, pt: (b, 0, 0)),
            pl.BlockSpec((None, t_new, d), lambda b, p, pt: (b, 0, 0)),
            pl.BlockSpec((rows, page), lambda b, p, pt: (0, p)),
            pl.BlockSpec((None, t_new, d), lambda b, p, pt: (b, 0, 0)),
            pl.BlockSpec(lamp.shape, lambda b, p, pt: (0, 0)),
            pl.BlockSpec(sg_row.shape, lambda b, p, pt: (0, 0)),
        ],
        out_specs=pl.BlockSpec((None, t_new, d), lambda b, p, pt: (b, 0, 0)),
        scratch_shapes=[pltpu.VMEM((rows, 1), F32), pltpu.VMEM((rows, 1), F32),
                        pltpu.VMEM((rows, d), F32), pltpu.VMEM((page, d), F32),
                        pltpu.VMEM((page, d), F32)],
    )
    out = pl.pallas_call(
        functools.partial(_attn_sample_kernel, n_pages=n_pages, t_new=t_new, lam_init=lam_init),
        grid_spec=grid_spec,
        out_shape=jax.ShapeDtypeStruct((db, t_new, d), F32),
        compiler_params=_cparams("parallel", "arbitrary"),
        name="attn_sample",
    )(page_flat, qbt, cache_k2, cache_v2, kn3, v3, bias, gz3, lamp, sg_row)
    return out.reshape(db * t_new, d)


def _pool_kernel(u_ref, hist_ref, gz_ref, w_ref, scale_ref, o_ref, ext_sc, *, tm, pos0, carry):
    c = pl.program_id(1)
    nb = u_ref.shape[0]
    grp = u_ref.shape[2] // len(POOL_WINDOWS)

    @pl.when(c == 0)
    def _hist():
        ext_sc[:, 0:POOL_HIST, :] = hist_ref[...]

    ext_sc[:, POOL_HIST:POOL_HIST + tm, :] = u_ref[...]
    pos = pos0 + c * tm + lax.broadcasted_iota(jnp.int32, (nb, tm, grp), 1)
    for gi, win in enumerate(POOL_WINDOWS):
        lanes = slice(gi * grp, (gi + 1) * grp)
        tot = ext_sc[:, POOL_HIST:POOL_HIST + tm, lanes]
        for j in range(1, win):
            tot = tot + ext_sc[:, POOL_HIST - j:POOL_HIST - j + tm, lanes]
        cnt = jnp.minimum(pos + 1, win).astype(F32)
        pooled = tot / cnt - u_ref[:, :, lanes]
        mixed = jnp.dot(pooled.reshape(nb * tm, grp).astype(BF16), w_ref[gi],
                        preferred_element_type=F32).reshape(nb, tm, grp)
        o_ref[:, :, lanes] = mixed * scale_ref[:, lanes] * gz_ref[:, :, lanes]
    if carry:
        ext_sc[:, 0:POOL_HIST, :] = ext_sc[:, tm:tm + POOL_HIST, :]


def _pool_call(u3, hist, gz3, pool_w_bf, pool_scale, pos0, nb):
    n, t, d = u3.shape
    tm = _tile(t, SEQ_TILE)
    nc = t // tm
    grp = d // len(POOL_WINDOWS)
    blk = lambda i, c: (i, c, 0)
    return pl.pallas_call(
        functools.partial(_pool_kernel, tm=tm, pos0=pos0, carry=nc > 1),
        grid=(n // nb, nc),
        in_specs=[pl.BlockSpec((nb, tm, d), blk),
                  pl.BlockSpec((nb, POOL_HIST, d), lambda i, c: (i, 0, 0)),
                  pl.BlockSpec((nb, tm, d), blk),
                  pl.BlockSpec((len(POOL_WINDOWS), grp, grp), lambda i, c: (0, 0, 0)),
                  pl.BlockSpec((1, d), lambda i, c: (0, 0))],
        out_specs=pl.BlockSpec((nb, tm, d), blk),
        out_shape=jax.ShapeDtypeStruct((n, t, d), F32),
        scratch_shapes=[pltpu.VMEM((nb, tm + POOL_HIST, d), F32)],
        compiler_params=_cparams("parallel", "arbitrary"),
        name="pool_mix",
    )(u3, hist, gz3, pool_w_bf, pool_scale)


def _lru_kernel(u_ref, hist_ref, h0_ref, gz_ref, cw_ref, cb_ref, wa_ref, wx_ref, ba_ref, bx_ref,
                lam_ref, o_ref, hn_ref, ext_sc, h_sc, *, tm, carry):
    c = pl.program_id(1)
    nb, _, d = u_ref.shape

    @pl.when(c == 0)
    def _state():
        ext_sc[:, 0:CONV_HIST, :] = hist_ref[...]
        h_sc[...] = h0_ref[...]

    ext_sc[:, CONV_HIST:CONV_HIST + tm, :] = u_ref[...]
    xc = cb_ref[...] + ext_sc[:, CONV_HIST - (CONV_WIDTH - 1):CONV_HIST - (CONV_WIDTH - 1) + tm, :] * cw_ref[0:1, :]
    for j in range(1, CONV_WIDTH):
        lo = CONV_HIST - (CONV_WIDTH - 1) + j
        xc = xc + ext_sc[:, lo:lo + tm, :] * cw_ref[j:j + 1, :]
    xb = xc.reshape(nb * tm, d).astype(BF16)
    nt = d // GATE_TILE
    ra = jnp.concatenate([jnp.dot(xb[:, j * GATE_TILE:(j + 1) * GATE_TILE], wa_ref[j],
                                  preferred_element_type=F32) for j in range(nt)], axis=1)
    rx = jnp.concatenate([jnp.dot(xb[:, j * GATE_TILE:(j + 1) * GATE_TILE], wx_ref[j],
                                  preferred_element_type=F32) for j in range(nt)], axis=1)
    r = jax.nn.sigmoid(ra + ba_ref[...]).reshape(nb, tm, d)
    i = jax.nn.sigmoid(rx + bx_ref[...]).reshape(nb, tm, d)
    log_a = (-LRU_C) * r * jax.nn.softplus(-lam_ref[...])
    a = jnp.exp(log_a)
    bt = jnp.sqrt(-jnp.tanh(log_a) * (a * a + 1.0)) * (i * xc)
    row = lax.broadcasted_iota(jnp.int32, (nb, tm, d), 1)
    sft = 1
    while sft < tm:
        keep = row >= sft
        a_prev = jnp.where(keep, pltpu.roll(a, sft, axis=1), 1.0)
        b_prev = jnp.where(keep, pltpu.roll(bt, sft, axis=1), 0.0)
        bt = a * b_prev + bt
        a = a * a_prev
        sft *= 2
    h = bt + a * h_sc[...]
    h_last = h[:, tm - 1:tm, :]
    h_sc[...] = h_last
    hn_ref[...] = h_last
    o_ref[...] = h * gz_ref[...]
    if carry:
        ext_sc[:, 0:CONV_HIST, :] = ext_sc[:, tm:tm + CONV_HIST, :]


def _lru_call(u3, hist, h0, gz3, conv_w, conv_b, wa, wx, ba, bx, lam, nb):
    n, t, d = u3.shape
    tm = _tile(t, SEQ_TILE)
    nc = t // tm
    blk = lambda i, c: (i, c, 0)
    full2 = lambda a: pl.BlockSpec(a.shape, lambda i, c: (0,) * a.ndim)
    return pl.pallas_call(
        functools.partial(_lru_kernel, tm=tm, carry=nc > 1),
        grid=(n // nb, nc),
        in_specs=[pl.BlockSpec((nb, tm, d), blk),
                  pl.BlockSpec((nb, CONV_HIST, d), lambda i, c: (i, 0, 0)),
                  pl.BlockSpec((nb, 1, d), lambda i, c: (i, 0, 0)),
                  pl.BlockSpec((nb, tm, d), blk),
                  full2(conv_w), full2(conv_b), full2(wa), full2(wx), full2(ba), full2(bx), full2(lam)],
        out_specs=[pl.BlockSpec((nb, tm, d), blk), pl.BlockSpec((nb, 1, d), lambda i, c: (i, 0, 0))],
        out_shape=[jax.ShapeDtypeStruct((n, t, d), F32), jax.ShapeDtypeStruct((n, 1, d), F32)],
        scratch_shapes=[pltpu.VMEM((nb, tm + CONV_HIST, d), F32), pltpu.VMEM((nb, 1, d), F32)],
        compiler_params=_cparams("parallel", "arbitrary"),
        name="conv_rglru",
    )(u3, hist, h0, gz3, conv_w, conv_b, wa, wx, ba, bx, lam)


def _out_kernel(x_ref, ma_ref, mp_ref, ml_ref, w_ref, g_ref, y_ref, xn_ref):
    m = (ma_ref[...] + mp_ref[...] + ml_ref[...]).astype(BF16)
    y = x_ref[...] + jnp.dot(m, w_ref[...], preferred_element_type=F32)
    y_ref[...] = y
    xn_ref[...] = _rms_rows(y, g_ref[...]).astype(BF16)


def _out_call(x, ma, mp, ml, w_out_bf, layer, g_next):
    r, d = x.shape
    tm = _tile(r, ROW_TILE // 2)
    spec = pl.BlockSpec((tm, d), lambda i: (i, 0))
    return pl.pallas_call(
        _out_kernel, grid=(r // tm,),
        in_specs=[spec, spec, spec, spec,
                  pl.BlockSpec((None, d, d), lambda i: (layer, 0, 0)),
                  pl.BlockSpec((1, d), lambda i: (0, 0))],
        out_specs=[spec, spec],
        out_shape=[jax.ShapeDtypeStruct((r, d), F32), jax.ShapeDtypeStruct((r, d), BF16)],
        compiler_params=_cparams("parallel"),
        name="out_proj",
    )(x, ma, mp, ml, w_out_bf, g_next)


def _block_diag(w, tile):
    n, c, _ = w.shape
    per = tile // c
    w4 = w.reshape(n // per, per, c, c)
    eye = jnp.eye(per, dtype=w.dtype)
    return (w4[:, :, :, None, :] * eye[None, :, None, :, None]).reshape(n // per, tile, tile)


def _mixers_and_out(x, xn, lw, layer, attn_fn, nseq, t, pos0, pool_hist, conv_hist, h0, nb, g_next):
    d = x.shape[-1]
    qn, kn, kb, v, vb, u, gz = _proj_calls(xn, lw["w_in"], layer, lw["pmat"], lw["gq"], lw["gk"])
    m_attn = attn_fn(qn, kn, kb, v, vb, gz[0])
    u_pool = u[0].reshape(nseq, t, d)
    u_lru = u[1].reshape(nseq, t, d)
    m_pool = _pool_call(u_pool, pool_hist, gz[1].reshape(nseq, t, d), lw["pool_w"], lw["pool_scale"],
                        pos0, nb)
    m_lru, h_new = _lru_call(u_lru, conv_hist, h0, gz[2].reshape(nseq, t, d), lw["conv_w"], lw["conv_b"],
                             lw["wa"], lw["wx"], lw["ba"], lw["bx"], lw["lam"], nb)
    y, xn_next = _out_call(x, m_attn, m_pool.reshape(nseq * t, d), m_lru.reshape(nseq * t, d),
                           lw["w_out"], layer, g_next)
    return y, xn_next, kn, v, u_pool, u_lru, h_new.reshape(nseq, d)


def kernel(x_prompt, x_sample, cache_k, cache_v, page_table, state_pool, state_conv, state_h, rel_bias, norm_g, w_in, q_norm_g, k_norm_g, lam_q1, lam_k1, lam_q2, lam_k2, subln_g, pool_w, pool_scale, conv_w, conv_b, gate_a_w, gate_a_b, gate_x_w, gate_x_b, lru_lambda, w_out):
    b, s, d = x_prompt.shape
    db, t_new, _ = x_sample.shape
    depth = w_in.shape[0]
    n_phys, page = cache_k.shape[1], cache_k.shape[2]
    n_pages = page_table.shape[1]
    past = n_pages * page
    assert d == N_HEADS * V_DIM and d % GATE_TILE == 0 and GATE_TILE % (d // LRU_BLOCKS) == 0

    w_in_bf = w_in.astype(BF16)
    w_out_bf = w_out.astype(BF16)
    pool_w_bf = pool_w.astype(BF16)
    pmat = jnp.kron(jnp.eye(d // HEAD_DIM, dtype=F32), jnp.ones((HEAD_DIM, HEAD_DIM), F32)).astype(BF16)
    reps = d // HEAD_DIM
    cache_k2 = cache_k.reshape(depth * n_phys, page, d)
    cache_v2 = cache_v.reshape(depth * n_phys, page, d)
    page_flat = page_table.reshape(-1).astype(jnp.int32)

    t_attn = _tile(s, ATTN_TILE)
    bvec_p = _bias_by_distance(rel_bias, 2 * t_attn)
    bias_tiles = _prompt_bias_tiles(bvec_p, t_attn)
    bias_s = _sample_bias(_bias_by_distance(rel_bias, past + t_new), past, page, t_new)

    pool0 = jnp.zeros((b, POOL_HIST, d), F32)
    conv0 = jnp.zeros((b, CONV_HIST, d), F32)
    h00 = jnp.zeros((b, 1, d), F32)
    nb_s = 16 if db % 16 == 0 else 1

    xp = x_prompt.reshape(b * s, d)
    xs = x_sample.reshape(db * t_new, d)
    xnp = _norm_call(xp, norm_g[0][None])
    xns = _norm_call(xs, norm_g[0][None])

    outs = [[] for _ in range(10)]
    for l in range(depth):
        lam_init = 0.8 - 0.6 * math.exp(-0.3 * l)
        lw = {
            "w_in": w_in_bf, "w_out": w_out_bf, "pmat": pmat,
            "gq": jnp.tile(q_norm_g[l] * (HEAD_DIM ** -0.5 * LOG2E), reps)[None],
            "gk": jnp.tile(k_norm_g[l], reps)[None],
            "pool_w": pool_w_bf[l], "pool_scale": pool_scale[l][None],
            "conv_w": conv_w[l], "conv_b": conv_b[l][None],
            "wa": _block_diag(gate_a_w[l], GATE_TILE).astype(BF16),
            "wx": _block_diag(gate_x_w[l], GATE_TILE).astype(BF16),
            "ba": gate_a_b[l].reshape(1, d), "bx": gate_x_b[l].reshape(1, d),
            "lam": lru_lambda[l][None],
        }
        lamp = jnp.stack([lam_q1[l], lam_k1[l], lam_q2[l], lam_k2[l]])
        g_next = norm_g[(l + 1) % depth][None]

        attn_p = lambda qn, kn, kb, v, vb, gza: _attn_prompt_call(
            qn, kb, vb, gza, bias_tiles, lamp, subln_g[l][:, None], lam_init, b, s)
        xp, xnp, kp, vp, up_pool, up_lru, hp = _mixers_and_out(
            xp, xnp, lw, l, attn_p, b, s, 0, pool0, conv0, h00, 1, g_next)

        attn_s = lambda qn, kn, kb, v, vb, gza: _attn_sample_call(
            qn, kn, v, gza, cache_k2, cache_v2, page_flat, l * n_phys, bias_s, lamp, subln_g[l][None],
            lam_init, db, t_new, n_pages, page)
        pool_hist = jnp.concatenate([jnp.zeros((db, POOL_HIST - POOL_BUF, d), F32), state_pool[l]], axis=1)
        conv_hist = jnp.concatenate([jnp.zeros((db, CONV_HIST - (CONV_WIDTH - 1), d), F32), state_conv[l]],
                                    axis=1)
        xs, xns, ksm, vsm, us_pool, us_lru, hs = _mixers_and_out(
            xs, xns, lw, l, attn_s, db, t_new, past, pool_hist, conv_hist, state_h[l][:, None], nb_s, g_next)

        new = (kp.reshape(b, s, N_HEADS, V_DIM), vp.reshape(b, s, N_HEADS, V_DIM),
               ksm.reshape(db, t_new, N_HEADS, V_DIM), vsm.reshape(db, t_new, N_HEADS, V_DIM),
               up_pool[:, -POOL_BUF:],
               jnp.concatenate([state_pool[l], us_pool], axis=1)[:, -POOL_BUF:],
               up_lru[:, -(CONV_WIDTH - 1):],
               jnp.concatenate([state_conv[l], us_lru], axis=1)[:, -(CONV_WIDTH - 1):],
               hp, hs)
        for acc, val in zip(outs, new):
            acc.append(val)
    return (xp.reshape(b, s, d), xs.reshape(db, t_new, d)) + tuple(jnp.stack(o) for o in outs)
```

```python
import functools
import math

import numpy as np
import jax
import jax.numpy as jnp
from jax import lax
from jax.experimental import pallas as pl
from jax.experimental.pallas import tpu as pltpu

F32 = jnp.float32
BF16 = jnp.bfloat16

RMS_EPS = 1e-6
NEG_INF = -1e30
LOG2E = 1.4426950408889634

N_HEADS = 8
HEAD_DIM = 64
V_DIM = 2 * HEAD_DIM
N_BUCKETS = 32
MAX_DISTANCE = 128
POOL_WINDOWS = (2, 4, 8, 16)
POOL_BUF = max(POOL_WINDOWS) - 1
POOL_HIST = 16
CONV_WIDTH = 4
CONV_HIST = 8
LRU_BLOCKS = 16
LRU_C = 8.0
N_BRANCH = 3
GATE_TILE = 256

ROW_TILE = 1024
ATTN_TILE = 512
ATTN_CHUNK = 256
SEQ_TILE = 256
SAMPLE_PAGES = 4
VMEM_LIMIT = 56 * 1024 * 1024


def _cparams(*sem):
    return pltpu.CompilerParams(dimension_semantics=sem, vmem_limit_bytes=VMEM_LIMIT)


def _tile(n, pref):
    return pref if n % pref == 0 else n


def _rms_rows(x, g):
    ms = jnp.mean(x * x, axis=-1, keepdims=True)
    return x * lax.rsqrt(ms + RMS_EPS) * g


def _norm_kernel(x_ref, g_ref, o_ref):
    o_ref[...] = _rms_rows(x_ref[...], g_ref[...]).astype(BF16)


def _norm_call(x, g):
    r, d = x.shape
    tm = _tile(r, ROW_TILE)
    return pl.pallas_call(
        _norm_kernel,
        grid=(r // tm,),
        in_specs=[pl.BlockSpec((tm, d), lambda i: (i, 0)), pl.BlockSpec((1, d), lambda i: (0, 0))],
        out_specs=pl.BlockSpec((tm, d), lambda i: (i, 0)),
        out_shape=jax.ShapeDtypeStruct((r, d), BF16),
        compiler_params=_cparams("parallel"),
        name="prenorm",
    )(x, g)


def _head_norm(acc, pmat_ref, g_ref):
    ss = jnp.dot((acc * acc).astype(BF16), pmat_ref[...], preferred_element_type=F32)
    return acc * lax.rsqrt(ss * (1.0 / HEAD_DIM) + RMS_EPS) * g_ref[...]


def _proj_q_kernel(xn_ref, w_ref, pmat_ref, g_ref, q_ref):
    acc = jnp.dot(xn_ref[...], w_ref[...], preferred_element_type=F32)
    q_ref[...] = _head_norm(acc, pmat_ref, g_ref).astype(BF16)


def _proj_k_kernel(xn_ref, w_ref, pmat_ref, g_ref, k_ref, kb_ref):
    acc = jnp.dot(xn_ref[...], w_ref[...], preferred_element_type=F32)
    kn = _head_norm(acc, pmat_ref, g_ref)
    k_ref[...] = kn
    kb_ref[...] = kn.astype(BF16)


def _proj_v_kernel(xn_ref, w_ref, v_ref, vb_ref):
    acc = jnp.dot(xn_ref[...], w_ref[...], preferred_element_type=F32)
    v_ref[...] = acc
    vb_ref[...] = acc.astype(BF16)


def _proj_u_kernel(xn_ref, w_ref, u_ref):
    u_ref[...] = jnp.dot(xn_ref[...], w_ref[...], preferred_element_type=F32)


def _proj_gz_kernel(xn_ref, wz_ref, wg_ref, o_ref):
    xn = xn_ref[...]
    z = jnp.dot(xn, wz_ref[...], preferred_element_type=F32)
    g = jnp.dot(xn, wg_ref[...], preferred_element_type=F32)
    o_ref[...] = jax.nn.sigmoid(g) * (z * jax.nn.sigmoid(z))


def _proj_calls(xn, w_bf, layer, pmat, gq, gk):
    r, d = xn.shape
    tm = _tile(r, ROW_TILE)
    nr = r // tm
    x_spec = pl.BlockSpec((tm, d), lambda i: (i, 0))
    full = lambda a: pl.BlockSpec(a.shape, lambda i: (0,) * a.ndim)
    o_spec = pl.BlockSpec((tm, d), lambda i: (i, 0))

    def w_spec(j):
        return pl.BlockSpec((None, d, d), lambda i: (layer, 0, j))

    qn = pl.pallas_call(
        _proj_q_kernel, grid=(nr,),
        in_specs=[x_spec, w_spec(0), full(pmat), full(gq)],
        out_specs=o_spec, out_shape=jax.ShapeDtypeStruct((r, d), BF16),
        compiler_params=_cparams("parallel"), name="proj_q")(xn, w_bf, pmat, gq)
    kn, kb = pl.pallas_call(
        _proj_k_kernel, grid=(nr,),
        in_specs=[x_spec, w_spec(1), full(pmat), full(gk)],
        out_specs=[o_spec, o_spec],
        out_shape=[jax.ShapeDtypeStruct((r, d), F32), jax.ShapeDtypeStruct((r, d), BF16)],
        compiler_params=_cparams("parallel"), name="proj_k")(xn, w_bf, pmat, gk)
    v, vb = pl.pallas_call(
        _proj_v_kernel, grid=(nr,),
        in_specs=[x_spec, w_spec(2)],
        out_specs=[o_spec, o_spec],
        out_shape=[jax.ShapeDtypeStruct((r, d), F32), jax.ShapeDtypeStruct((r, d), BF16)],
        compiler_params=_cparams("parallel"), name="proj_v")(xn, w_bf)
    u = pl.pallas_call(
        _proj_u_kernel, grid=(2, nr),
        in_specs=[pl.BlockSpec((tm, d), lambda j, i: (i, 0)),
                  pl.BlockSpec((None, d, d), lambda j, i: (layer, 0, 3 + j))],
        out_specs=pl.BlockSpec((None, tm, d), lambda j, i: (j, i, 0)),
        out_shape=jax.ShapeDtypeStruct((2, r, d), F32),
        compiler_params=_cparams("parallel", "parallel"), name="proj_u")(xn, w_bf)
    gz = pl.pallas_call(
        _proj_gz_kernel, grid=(N_BRANCH, nr),
        in_specs=[pl.BlockSpec((tm, d), lambda j, i: (i, 0)),
                  pl.BlockSpec((None, d, d), lambda j, i: (layer, 0, 5 + j)),
                  pl.BlockSpec((None, d, d), lambda j, i: (layer, 0, 8 + j))],
        out_specs=pl.BlockSpec((None, tm, d), lambda j, i: (j, i, 0)),
        out_shape=jax.ShapeDtypeStruct((N_BRANCH, r, d), F32),
        compiler_params=_cparams("parallel", "parallel"), name="proj_gz")(xn, w_bf, w_bf)
    return qn, kn, kb, v, vb, u, gz


def _lam_value(lamp_ref, lam_init):
    lp = lamp_ref[...]
    s1 = jnp.sum(lp[0:1] * lp[1:2], axis=1, keepdims=True)
    s2 = jnp.sum(lp[2:3] * lp[3:4], axis=1, keepdims=True)
    return jnp.exp(s1) - jnp.exp(s2) + lam_init


def _bias_by_distance(rel_bias, n_max):
    n = jnp.arange(n_max, dtype=jnp.int32)
    max_exact = N_BUCKETS // 2
    nf = jnp.maximum(n, 1).astype(F32)
    large = max_exact + (jnp.log(nf / max_exact) / math.log(MAX_DISTANCE / max_exact)
                         * (N_BUCKETS - max_exact)).astype(jnp.int32)
    large = jnp.minimum(large, N_BUCKETS - 1)
    return rel_bias[jnp.where(n < max_exact, n, large)]


def _far_distance():
    max_exact = N_BUCKETS // 2
    n = np.arange(1, 4 * MAX_DISTANCE, dtype=np.int64)
    large = max_exact + (np.log(n.astype(np.float32) / np.float32(max_exact))
                         / np.float32(math.log(MAX_DISTANCE / max_exact))
                         * (N_BUCKETS - max_exact)).astype(np.int64)
    not_last = np.nonzero((n < max_exact) | (large < N_BUCKETS - 1))[0]
    return int(n[not_last[-1]] + 1) + 2


def _attn_prompt_kernel(q_ref, k_ref, v_ref, bias_ref, gz_ref, lamp_ref, sg_ref, o_ref,
                        vT_sc, qp_sc, s_sc, m_sc, l_sc, acc_sc, *, t, n_blk, lam_init):
    qi = pl.program_id(2)

    @pl.when(qi == 0)
    def _transpose_values():
        def body(c, carry):
            r0 = pl.multiple_of(c * t, t)
            vT_sc[:, pl.ds(r0, t)] = v_ref[pl.ds(r0, t), :].astype(F32).T.astype(BF16)
            return carry
        lax.fori_loop(0, n_blk, body, 0)

    qT = q_ref[...].astype(F32).T
    row = lax.broadcasted_iota(jnp.int32, qT.shape, 0)
    qp_sc[:, :t] = jnp.where(row < HEAD_DIM, qT, 0.0).astype(BF16)
    qp_sc[:, t:] = jnp.where(row >= HEAD_DIM, qT, 0.0).astype(BF16)
    m_sc[...] = jnp.full(m_sc.shape, NEG_INF, F32)
    l_sc[...] = jnp.zeros(l_sc.shape, F32)
    acc_sc[...] = jnp.zeros(acc_sc.shape, F32)

    def scores(ki, cj, bias_idx):
        r0 = pl.multiple_of(ki * t, t)
        s = jnp.dot(k_ref[pl.ds(r0, t), :], qp_sc[:, cj * t:(cj + 1) * t], preferred_element_type=F32)
        if bias_idx is not None:
            s = s + bias_ref[bias_idx]
        s_sc[cj] = s

    def update(ki, cj):
        r0 = pl.multiple_of(ki * t, t)
        cols = slice(cj * t, (cj + 1) * t)
        s = s_sc[cj]
        m_old = m_sc[:, cols]
        m_new = jnp.maximum(m_old, jnp.max(s, axis=0, keepdims=True))
        alpha = jnp.exp2(m_old - m_new)
        p = jnp.exp2(s - m_new)
        l_sc[:, cols] = alpha * l_sc[:, cols] + jnp.sum(p, axis=0, keepdims=True)
        acc_sc[:, cols] = alpha * acc_sc[:, cols] + jnp.dot(
            vT_sc[:, pl.ds(r0, t)], p.astype(BF16), preferred_element_type=F32)
        m_sc[:, cols] = m_new

    scores(0, 0, None)

    def far_block(ki):
        scores(ki, 1, None)
        update(ki, 0)
        scores(ki + 1, 0, None)
        update(ki, 1)

    def far_pair(j, carry):
        far_block(2 * j)
        far_block(2 * j + 1)
        return carry

    n_far = jnp.maximum(qi - 1, 0)
    lax.fori_loop(0, n_far // 2, far_pair, 0)

    @pl.when(n_far % 2 == 1)
    def _odd_far_block():
        far_block(n_far - 1)

    @pl.when(qi >= 1)
    def _below_diagonal():
        s_sc[0] = s_sc[0] + bias_ref[1]
        scores(qi - 1, 1, 1)
        update(qi - 1, 0)
        scores(qi, 0, 0)
        update(qi - 1, 1)

    @pl.when(qi == 0)
    def _first_block():
        s_sc[0] = s_sc[0] + bias_ref[0]

    scores(qi, 1, 0)
    update(qi, 0)
    update(qi, 1)

    lam = _lam_value(lamp_ref, lam_init)
    on = acc_sc[...] * (1.0 / l_sc[...])
    o = on[:, :t] - lam * on[:, t:]
    ms = jnp.mean(o * o, axis=0, keepdims=True)
    y = o * lax.rsqrt(ms + RMS_EPS) * sg_ref[...] * (1.0 - lam_init)
    o_ref[...] = y.T * gz_ref[...]


def _toeplitz(rel_t, t):
    h, l = rel_t.shape
    padded = jnp.concatenate([rel_t, jnp.zeros((h, 1), rel_t.dtype)], axis=1)
    skew = jnp.broadcast_to(padded[:, None, :], (h, t, l + 1)).reshape(h, t * (l + 1))
    return skew[:, :t * l].reshape(h, t, l)


def _prompt_bias_tiles(bvec, t):
    rel_t = ((bvec - bvec[-1][None, :]) * LOG2E).T
    toe = _toeplitz(rel_t, t)
    kk = jnp.arange(t, dtype=jnp.int32)[:, None]
    qq = jnp.arange(t, dtype=jnp.int32)[None, :]
    diag = jnp.where((qq >= kk)[None], toe[:, :, :t], NEG_INF)
    return jnp.stack([diag, toe[:, :, t:]], axis=1)


def _attn_prompt_call(qn, kb, vb, gz, bias_tiles, lamp, sg_col, lam_init, b, s):
    d = qn.shape[-1]
    t = _tile(s, ATTN_TILE)
    assert t + 1 >= _far_distance(), "key blocks two or more below the diagonal must sit in the last bias bucket"
    q3, k3, v3 = (a.reshape(b, s, d) for a in (qn, kb, vb))
    gz4 = gz.reshape(N_BRANCH, b, s, d)
    out = pl.pallas_call(
        functools.partial(_attn_prompt_kernel, t=t, n_blk=s // t, lam_init=lam_init),
        grid=(b, N_HEADS, s // t),
        in_specs=[
            pl.BlockSpec((None, t, V_DIM), lambda bi, h, qi: (bi, qi, h)),
            pl.BlockSpec((None, s, V_DIM), lambda bi, h, qi: (bi, 0, h)),
            pl.BlockSpec((None, s, V_DIM), lambda bi, h, qi: (bi, 0, h)),
            pl.BlockSpec((None, 2, t, t), lambda bi, h, qi: (h, 0, 0, 0)),
            pl.BlockSpec((None, None, t, V_DIM), lambda bi, h, qi: (0, bi, qi, h)),
            pl.BlockSpec(lamp.shape, lambda bi, h, qi: (0, 0)),
            pl.BlockSpec(sg_col.shape, lambda bi, h, qi: (0, 0)),
        ],
        out_specs=pl.BlockSpec((None, t, V_DIM), lambda bi, h, qi: (bi, qi, h)),
        out_shape=jax.ShapeDtypeStruct((b, s, d), F32),
        scratch_shapes=[pltpu.VMEM((V_DIM, s), BF16), pltpu.VMEM((V_DIM, 2 * t), BF16),
                        pltpu.VMEM((2, t, t), F32),
                        pltpu.VMEM((1, 2 * t), F32), pltpu.VMEM((1, 2 * t), F32),
                        pltpu.VMEM((V_DIM, 2 * t), F32)],
        compiler_params=_cparams("parallel", "parallel", "arbitrary"),
        name="attn_prompt",
    )(q3, k3, v3, bias_tiles, gz4, lamp, sg_col)
    return out.reshape(b * s, d)


def _attn_sample_kernel(pt_ref, qbd_ref, *refs, pp, page, n_steps, t_new, lam_init):
    kc = refs[:pp]
    vc = refs[pp:2 * pp]
    (kn_ref, vn_ref, bias_ref, biasn_ref, gz_ref, lamp_ref, sg_ref, o_ref,
     m_sc, l_sc, acc_sc, kp_sc, vp_sc) = refs[2 * pp:]
    p = pl.program_id(1)
    grp = 2 * t_new

    @pl.when(p == 0)
    def _init():
        m_sc[...] = jnp.full(m_sc.shape, NEG_INF, F32)
        l_sc[...] = jnp.zeros(l_sc.shape, F32)
        acc_sc[...] = jnp.zeros(acc_sc.shape, F32)

    def one_page(k_head, v_head, bias):
        s = jnp.concatenate(
            [lax.dot_general(qbd_ref[h * grp:(h + 1) * grp, :], k_head(h).astype(BF16),
                             (((1,), (1,)), ((), ())), preferred_element_type=F32)
             for h in range(N_HEADS)], axis=0) + bias
        m_old = m_sc[...]
        m_new = jnp.maximum(m_old, jnp.max(s, axis=1, keepdims=True))
        alpha = jnp.exp2(m_old - m_new)
        pr = jnp.exp2(s - m_new)
        l_sc[...] = alpha * l_sc[...] + jnp.sum(pr, axis=1, keepdims=True)
        prb = pr.astype(BF16)
        pv = jnp.concatenate(
            [jnp.dot(prb[h * grp:(h + 1) * grp, :], v_head(h).astype(BF16), preferred_element_type=F32)
             for h in range(N_HEADS)], axis=0)
        acc_sc[...] = alpha * acc_sc[...] + pv
        m_sc[...] = m_new

    for j in range(pp):
        one_page(lambda h, j=j: kc[j][pl.ds(h, page, stride=N_HEADS), :],
                 lambda h, j=j: vc[j][pl.ds(h, page, stride=N_HEADS), :],
                 bias_ref[:, j * page:(j + 1) * page])

    @pl.when(p == n_steps - 1)
    def _new_tokens():
        kp_sc[...] = jnp.zeros(kp_sc.shape, F32)
        vp_sc[...] = jnp.zeros(vp_sc.shape, F32)
        kp_sc[0:t_new, :] = kn_ref[...]
        vp_sc[0:t_new, :] = vn_ref[...]
        one_page(lambda h: kp_sc[:, h * V_DIM:(h + 1) * V_DIM],
                 lambda h: vp_sc[:, h * V_DIM:(h + 1) * V_DIM], biasn_ref[...])
        lam = _lam_value(lamp_ref, lam_init)
        on = acc_sc[...] * (1.0 / l_sc[...])
        for h in range(N_HEADS):
            r1 = h * grp
            o = on[r1:r1 + t_new, :] - lam * on[r1 + t_new:r1 + grp, :]
            ms = jnp.mean(o * o, axis=1, keepdims=True)
            y = o * lax.rsqrt(ms + RMS_EPS) * sg_ref[...] * (1.0 - lam_init)
            lanes = slice(h * V_DIM, (h + 1) * V_DIM)
            o_ref[:, lanes] = y * gz_ref[:, lanes]


def _sample_bias(bvec, past, page, t_new):
    n, h = bvec.shape
    rev = jnp.concatenate([bvec[::-1] * LOG2E, jnp.full((page, h), NEG_INF, F32)], axis=0)
    rows = jnp.stack([rev[t_new - 1 - tt:t_new - 1 - tt + past + page] for tt in range(t_new)])
    vals = jnp.transpose(rows, (2, 0, 1))
    vals = jnp.broadcast_to(vals[:, None], (h, 2, t_new, past + page))
    return vals.reshape(h * 2 * t_new, past + page)


def _attn_sample_call(qn, kn, v, gz, cache_k4, cache_v4, page_flat, layer, bias, lamp, sg_row,
                      lam_init, db, t_new, n_pages, page):
    d = qn.shape[-1]
    rows = 2 * N_HEADS * t_new
    pp = SAMPLE_PAGES if n_pages % SAMPLE_PAGES == 0 else 1
    n_steps = n_pages // pp
    past = n_pages * page
    q5 = jnp.transpose(qn.reshape(db, t_new, N_HEADS, 2, HEAD_DIM), (0, 2, 3, 1, 4))
    eye = jnp.eye(2, dtype=BF16)
    qbd = (q5[:, :, :, :, None, :] * eye[None, None, :, None, :, None]).reshape(db, rows, V_DIM)
    kn3 = kn.reshape(db, t_new, d)
    v3 = v.reshape(db, t_new, d)
    gz4 = gz.reshape(N_BRANCH, db, t_new, d)

    def page_spec(j):
        return pl.BlockSpec((None, None, page * N_HEADS, V_DIM),
                            lambda b, p, pt, j=j: (layer, pt[b * n_pages + p * pp + j], 0, 0))

    tok_spec = pl.BlockSpec((None, t_new, d), lambda b, p, pt: (b, 0, 0))
    grid_spec = pltpu.PrefetchScalarGridSpec(
        num_scalar_prefetch=1,
        grid=(db, n_steps),
        in_specs=[pl.BlockSpec((None, rows, V_DIM), lambda b, p, pt: (b, 0, 0))]
        + [page_spec(j) for j in range(pp)] + [page_spec(j) for j in range(pp)]
        + [tok_spec, tok_spec,
           pl.BlockSpec((rows, pp * page), lambda b, p, pt: (0, p)),
           pl.BlockSpec((rows, page), lambda b, p, pt: (0, past // page)),
           pl.BlockSpec((None, None, t_new, d), lambda b, p, pt: (0, b, 0, 0)),
           pl.BlockSpec(lamp.shape, lambda b, p, pt: (0, 0)),
           pl.BlockSpec(sg_row.shape, lambda b, p, pt: (0, 0))],
        out_specs=pl.BlockSpec((None, t_new, d), lambda b, p, pt: (b, 0, 0)),
        scratch_shapes=[pltpu.VMEM((rows, 1), F32), pltpu.VMEM((rows, 1), F32),
                        pltpu.VMEM((rows, V_DIM), F32), pltpu.VMEM((page, d), F32),
                        pltpu.VMEM((page, d), F32)],
    )
    out = pl.pallas_call(
        functools.partial(_attn_sample_kernel, pp=pp, page=page, n_steps=n_steps, t_new=t_new,
                          lam_init=lam_init),
        grid_spec=grid_spec,
        out_shape=jax.ShapeDtypeStruct((db, t_new, d), F32),
        compiler_params=_cparams("parallel", "arbitrary"),
        name="attn_sample",
    )(page_flat, qbd, *([cache_k4] * pp), *([cache_v4] * pp), kn3, v3, bias, bias, gz4, lamp, sg_row)
    return out.reshape(db * t_new, d)


def _pool_kernel(u_ref, hist_ref, gz_ref, w_ref, scale_ref, o_ref, ext_sc, *, tm, pos0, carry):
    c = pl.program_id(1)
    nb = u_ref.shape[0]
    grp = u_ref.shape[2] // len(POOL_WINDOWS)

    @pl.when(c == 0)
    def _hist():
        ext_sc[:, 0:POOL_HIST, :] = hist_ref[...]

    ext_sc[:, POOL_HIST:POOL_HIST + tm, :] = u_ref[...]
    pos = pos0 + c * tm + lax.broadcasted_iota(jnp.int32, (nb, tm, grp), 1)
    for gi, win in enumerate(POOL_WINDOWS):
        lanes = slice(gi * grp, (gi + 1) * grp)
        tot = ext_sc[:, POOL_HIST:POOL_HIST + tm, lanes]
        for j in range(1, win):
            tot = tot + ext_sc[:, POOL_HIST - j:POOL_HIST - j + tm, lanes]
        cnt = jnp.minimum(pos + 1, win).astype(F32)
        pooled = tot / cnt - u_ref[:, :, lanes]
        mixed = jnp.dot(pooled.reshape(nb * tm, grp).astype(BF16), w_ref[gi],
                        preferred_element_type=F32).reshape(nb, tm, grp)
        o_ref[:, :, lanes] = mixed * scale_ref[:, lanes] * gz_ref[:, :, lanes]
    if carry:
        ext_sc[:, 0:POOL_HIST, :] = ext_sc[:, tm:tm + POOL_HIST, :]


def _pool_call(u4, hist, gz4, pool_w_bf, pool_scale, pos0, nb):
    _, n, t, d = u4.shape
    tm = _tile(t, SEQ_TILE)
    nc = t // tm
    grp = d // len(POOL_WINDOWS)
    blk = lambda i, c: (i, c, 0)
    return pl.pallas_call(
        functools.partial(_pool_kernel, tm=tm, pos0=pos0, carry=nc > 1),
        grid=(n // nb, nc),
        in_specs=[pl.BlockSpec((None, nb, tm, d), lambda i, c: (0, i, c, 0)),
                  pl.BlockSpec((nb, POOL_HIST, d), lambda i, c: (i, 0, 0)),
                  pl.BlockSpec((None, nb, tm, d), lambda i, c: (1, i, c, 0)),
                  pl.BlockSpec((len(POOL_WINDOWS), grp, grp), lambda i, c: (0, 0, 0)),
                  pl.BlockSpec((1, d), lambda i, c: (0, 0))],
        out_specs=pl.BlockSpec((nb, tm, d), blk),
        out_shape=jax.ShapeDtypeStruct((n, t, d), F32),
        scratch_shapes=[pltpu.VMEM((nb, tm + POOL_HIST, d), F32)],
        compiler_params=_cparams("parallel", "arbitrary"),
        name="pool_mix",
    )(u4, hist, gz4, pool_w_bf, pool_scale)


def _lru_kernel(u_ref, hist_ref, h0_ref, gz_ref, cw_ref, cb_ref, wa_ref, wx_ref, ba_ref, bx_ref,
                lam_ref, o_ref, hn_ref, ext_sc, h_sc, *, tm, carry):
    c = pl.program_id(1)
    nb, _, d = u_ref.shape

    @pl.when(c == 0)
    def _state():
        ext_sc[:, 0:CONV_HIST, :] = hist_ref[...]
        h_sc[...] = h0_ref[...]

    ext_sc[:, CONV_HIST:CONV_HIST + tm, :] = u_ref[...]
    xc = cb_ref[...] + ext_sc[:, CONV_HIST - (CONV_WIDTH - 1):CONV_HIST - (CONV_WIDTH - 1) + tm, :] * cw_ref[0:1, :]
    for j in range(1, CONV_WIDTH):
        lo = CONV_HIST - (CONV_WIDTH - 1) + j
        xc = xc + ext_sc[:, lo:lo + tm, :] * cw_ref[j:j + 1, :]
    xb = xc.reshape(nb * tm, d).astype(BF16)
    nt = d // GATE_TILE
    ra = jnp.concatenate([jnp.dot(xb[:, j * GATE_TILE:(j + 1) * GATE_TILE], wa_ref[j],
                                  preferred_element_type=F32) for j in range(nt)], axis=1)
    rx = jnp.concatenate([jnp.dot(xb[:, j * GATE_TILE:(j + 1) * GATE_TILE], wx_ref[j],
                                  preferred_element_type=F32) for j in range(nt)], axis=1)
    r = jax.nn.sigmoid(ra + ba_ref[...]).reshape(nb, tm, d)
    i = jax.nn.sigmoid(rx + bx_ref[...]).reshape(nb, tm, d)
    log_a = (-LRU_C) * r * jax.nn.softplus(-lam_ref[...])
    a = jnp.exp(log_a)
    bt = jnp.sqrt(-jnp.tanh(log_a) * (a * a + 1.0)) * (i * xc)
    row = lax.broadcasted_iota(jnp.int32, (nb, tm, d), 1)
    sft = 1
    while sft < tm:
        keep = row >= sft
        a_prev = jnp.where(keep, pltpu.roll(a, sft, axis=1), 1.0)
        b_prev = jnp.where(keep, pltpu.roll(bt, sft, axis=1), 0.0)
        bt = a * b_prev + bt
        a = a * a_prev
        sft *= 2
    h = bt + a * h_sc[...]
    h_last = h[:, tm - 1:tm, :]
    h_sc[...] = h_last
    hn_ref[...] = h_last
    o_ref[...] = h * gz_ref[...]
    if carry:
        ext_sc[:, 0:CONV_HIST, :] = ext_sc[:, tm:tm + CONV_HIST, :]


def _lru_call(u4, hist, h0, gz4, conv_w, conv_b, wa, wx, ba, bx, lam, nb):
    _, n, t, d = u4.shape
    tm = _tile(t, SEQ_TILE)
    nc = t // tm
    blk = lambda i, c: (i, c, 0)
    full2 = lambda a: pl.BlockSpec(a.shape, lambda i, c: (0,) * a.ndim)
    return pl.pallas_call(
        functools.partial(_lru_kernel, tm=tm, carry=nc > 1),
        grid=(n // nb, nc),
        in_specs=[pl.BlockSpec((None, nb, tm, d), lambda i, c: (1, i, c, 0)),
                  pl.BlockSpec((nb, CONV_HIST, d), lambda i, c: (i, 0, 0)),
                  pl.BlockSpec((nb, 1, d), lambda i, c: (i, 0, 0)),
                  pl.BlockSpec((None, nb, tm, d), lambda i, c: (2, i, c, 0)),
                  full2(conv_w), full2(conv_b), full2(wa), full2(wx), full2(ba), full2(bx), full2(lam)],
        out_specs=[pl.BlockSpec((nb, tm, d), blk), pl.BlockSpec((nb, 1, d), lambda i, c: (i, 0, 0))],
        out_shape=[jax.ShapeDtypeStruct((n, t, d), F32), jax.ShapeDtypeStruct((n, 1, d), F32)],
        scratch_shapes=[pltpu.VMEM((nb, tm + CONV_HIST, d), F32), pltpu.VMEM((nb, 1, d), F32)],
        compiler_params=_cparams("parallel", "arbitrary"),
        name="conv_rglru",
    )(u4, hist, h0, gz4, conv_w, conv_b, wa, wx, ba, bx, lam)


def _out_kernel(x_ref, ma_ref, mp_ref, ml_ref, w_ref, g_ref, y_ref, xn_ref):
    m = (ma_ref[...] + mp_ref[...] + ml_ref[...]).astype(BF16)
    y = x_ref[...] + jnp.dot(m, w_ref[...], preferred_element_type=F32)
    y_ref[...] = y
    xn_ref[...] = _rms_rows(y, g_ref[...]).astype(BF16)


def _out_call(x, ma, mp, ml, w_out_bf, layer, g_next):
    r, d = x.shape
    tm = _tile(r, ROW_TILE // 2)
    spec = pl.BlockSpec((tm, d), lambda i: (i, 0))
    return pl.pallas_call(
        _out_kernel, grid=(r // tm,),
        in_specs=[spec, spec, spec, spec,
                  pl.BlockSpec((None, d, d), lambda i: (layer, 0, 0)),
                  pl.BlockSpec((1, d), lambda i: (0, 0))],
        out_specs=[spec, spec],
        out_shape=[jax.ShapeDtypeStruct((r, d), F32), jax.ShapeDtypeStruct((r, d), BF16)],
        compiler_params=_cparams("parallel"),
        name="out_proj",
    )(x, ma, mp, ml, w_out_bf, g_next)


def _block_diag(w, tile):
    n, c, _ = w.shape
    per = tile // c
    w4 = w.reshape(n // per, per, c, c)
    eye = jnp.eye(per, dtype=w.dtype)
    return (w4[:, :, :, None, :] * eye[None, :, None, :, None]).reshape(n // per, tile, tile)


def _mixers_and_out(x, xn, lw, layer, attn_fn, nseq, t, pos0, pool_hist, conv_hist, h0, nb, g_next):
    d = x.shape[-1]
    qn, kn, kb, v, vb, u, gz = _proj_calls(xn, lw["w_in"], layer, lw["pmat"], lw["gq"], lw["gk"])
    m_attn = attn_fn(qn, kn, kb, v, vb, gz)
    u4 = u.reshape(2, nseq, t, d)
    gz4 = gz.reshape(N_BRANCH, nseq, t, d)
    m_pool = _pool_call(u4, pool_hist, gz4, lw["pool_w"], lw["pool_scale"], pos0, nb)
    m_lru, h_new = _lru_call(u4, conv_hist, h0, gz4, lw["conv_w"], lw["conv_b"],
                             lw["wa"], lw["wx"], lw["ba"], lw["bx"], lw["lam"], nb)
    y, xn_next = _out_call(x, m_attn, m_pool.reshape(nseq * t, d), m_lru.reshape(nseq * t, d),
                           lw["w_out"], layer, g_next)
    return y, xn_next, kn, v, u4, h_new.reshape(nseq, d)


def kernel(x_prompt, x_sample, cache_k, cache_v, page_table, state_pool, state_conv, state_h, rel_bias, norm_g, w_in, q_norm_g, k_norm_g, lam_q1, lam_k1, lam_q2, lam_k2, subln_g, pool_w, pool_scale, conv_w, conv_b, gate_a_w, gate_a_b, gate_x_w, gate_x_b, lru_lambda, w_out):
    b, s, d = x_prompt.shape
    db, t_new, _ = x_sample.shape
    depth = w_in.shape[0]
    n_phys, page = cache_k.shape[1], cache_k.shape[2]
    n_pages = page_table.shape[1]
    past = n_pages * page
    assert d == N_HEADS * V_DIM and d % GATE_TILE == 0 and GATE_TILE % (d // LRU_BLOCKS) == 0
    assert s >= POOL_BUF and cache_k.shape[3:] == (N_HEADS, V_DIM)

    w_in_bf = w_in.astype(BF16)
    w_out_bf = w_out.astype(BF16)
    pool_w_bf = pool_w.astype(BF16)
    pmat = jnp.kron(jnp.eye(d // HEAD_DIM, dtype=F32), jnp.ones((HEAD_DIM, HEAD_DIM), F32)).astype(BF16)
    reps = d // HEAD_DIM
    cache_k4 = cache_k.reshape(depth, n_phys, page * N_HEADS, V_DIM)
    cache_v4 = cache_v.reshape(depth, n_phys, page * N_HEADS, V_DIM)
    page_flat = page_table.reshape(-1).astype(jnp.int32)

    t_attn = _tile(s, ATTN_TILE)
    bias_tiles = _prompt_bias_tiles(_bias_by_distance(rel_bias, 2 * t_attn), t_attn)
    bias_s = _sample_bias(_bias_by_distance(rel_bias, past + t_new), past, page, t_new)

    pool0 = jnp.zeros((b, POOL_HIST, d), F32)
    conv0 = jnp.zeros((b, CONV_HIST, d), F32)
    h00 = jnp.zeros((b, 1, d), F32)
    nb_s = 16 if db % 16 == 0 else 1

    xp = x_prompt.reshape(b * s, d)
    xs = x_sample.reshape(db * t_new, d)
    xnp = _norm_call(xp, norm_g[0][None])
    xns = _norm_call(xs, norm_g[0][None])

    outs = [[] for _ in range(10)]
    for l in range(depth):
        lam_init = 0.8 - 0.6 * math.exp(-0.3 * l)
        lw = {
            "w_in": w_in_bf, "w_out": w_out_bf, "pmat": pmat,
            "gq": jnp.tile(q_norm_g[l] * (HEAD_DIM ** -0.5 * LOG2E), reps)[None],
            "gk": jnp.tile(k_norm_g[l], reps)[None],
            "pool_w": pool_w_bf[l], "pool_scale": pool_scale[l][None],
            "conv_w": conv_w[l], "conv_b": conv_b[l][None],
            "wa": _block_diag(gate_a_w[l], GATE_TILE).astype(BF16),
            "wx": _block_diag(gate_x_w[l], GATE_TILE).astype(BF16),
            "ba": gate_a_b[l].reshape(1, d), "bx": gate_x_b[l].reshape(1, d),
            "lam": lru_lambda[l][None],
        }
        lamp = jnp.stack([lam_q1[l], lam_k1[l], lam_q2[l], lam_k2[l]])
        g_next = norm_g[(l + 1) % depth][None]

        attn_p = lambda qn, kn, kb, v, vb, gz: _attn_prompt_call(
            qn, kb, vb, gz, bias_tiles, lamp, subln_g[l][:, None], lam_init, b, s)
        xp, xnp, kp, vp, up4, hp = _mixers_and_out(
            xp, xnp, lw, l, attn_p, b, s, 0, pool0, conv0, h00, 1, g_next)

        attn_s = lambda qn, kn, kb, v, vb, gz: _attn_sample_call(
            qn, kn, v, gz, cache_k4, cache_v4, page_flat, l, bias_s, lamp, subln_g[l][None],
            lam_init, db, t_new, n_pages, page)
        pool_hist = jnp.concatenate([jnp.zeros((db, POOL_HIST - POOL_BUF, d), F32), state_pool[l]], axis=1)
        conv_hist = jnp.concatenate([jnp.zeros((db, CONV_HIST - (CONV_WIDTH - 1), d), F32), state_conv[l]],
                                    axis=1)
        xs, xns, ksm, vsm, us4, hs = _mixers_and_out(
            xs, xns, lw, l, attn_s, db, t_new, past, pool_hist, conv_hist, state_h[l][:, None], nb_s, g_next)

        new = (kp.reshape(b, s, N_HEADS, V_DIM), vp.reshape(b, s, N_HEADS, V_DIM),
               ksm.reshape(db, t_new, N_HEADS, V_DIM), vsm.reshape(db, t_new, N_HEADS, V_DIM),
               up4[0, :, -POOL_BUF:],
               jnp.concatenate([state_pool[l], us4[0]], axis=1)[:, -POOL_BUF:],
               up4[1, :, -(CONV_WIDTH - 1):],
               jnp.concatenate([state_conv[l], us4[1]], axis=1)[:, -(CONV_WIDTH - 1):],
               hp, hs)
        for acc, val in zip(outs, new):
            acc.append(val)
    return (xp.reshape(b, s, d), xs.reshape(db, t_new, d)) + tuple(jnp.stack(o) for o in outs)
```

```python
import functools
import math

import numpy as np
import jax
import jax.numpy as jnp
from jax import lax
from jax.experimental import pallas as pl
from jax.experimental.pallas import tpu as pltpu

F32 = jnp.float32
BF16 = jnp.bfloat16

RMS_EPS = 1e-6
NEG_INF = -1e30
LOG2E = 1.4426950408889634

N_HEADS = 8
HEAD_DIM = 64
V_DIM = 2 * HEAD_DIM
N_BUCKETS = 32
MAX_DISTANCE = 128
POOL_WINDOWS = (2, 4, 8, 16)
POOL_BUF = max(POOL_WINDOWS) - 1
POOL_HIST = 16
CONV_WIDTH = 4
CONV_HIST = 8
LRU_BLOCKS = 16
LRU_C = 8.0
N_BRANCH = 3
GATE_TILE = 256

ROW_TILE = 1024
ATTN_TILE = 512
ATTN_CHUNK = 256
SEQ_TILE = 256
SCAN_GROUP = 8
SAMPLE_PAGES = 4
VMEM_LIMIT = 56 * 1024 * 1024


def _cparams(*sem):
    return pltpu.CompilerParams(dimension_semantics=sem, vmem_limit_bytes=VMEM_LIMIT)


def _tile(n, pref):
    return pref if n % pref == 0 else n


def _rms_rows(x, g):
    ms = jnp.mean(x * x, axis=-1, keepdims=True)
    return x * lax.rsqrt(ms + RMS_EPS) * g


def _norm_kernel(x_ref, g_ref, o_ref):
    o_ref[...] = _rms_rows(x_ref[...], g_ref[...]).astype(BF16)


def _norm_call(x, g):
    r, d = x.shape
    tm = _tile(r, ROW_TILE)
    return pl.pallas_call(
        _norm_kernel,
        grid=(r // tm,),
        in_specs=[pl.BlockSpec((tm, d), lambda i: (i, 0)), pl.BlockSpec((1, d), lambda i: (0, 0))],
        out_specs=pl.BlockSpec((tm, d), lambda i: (i, 0)),
        out_shape=jax.ShapeDtypeStruct((r, d), BF16),
        compiler_params=_cparams("parallel"),
        name="prenorm",
    )(x, g)


def _head_norm(acc, pmat_ref, g_ref):
    ss = jnp.dot((acc * acc).astype(BF16), pmat_ref[...], preferred_element_type=F32)
    return acc * lax.rsqrt(ss * (1.0 / HEAD_DIM) + RMS_EPS) * g_ref[...]


def _proj_q_kernel(xn_ref, w_ref, pmat_ref, g_ref, q_ref):
    acc = jnp.dot(xn_ref[...], w_ref[...], preferred_element_type=F32)
    q_ref[...] = _head_norm(acc, pmat_ref, g_ref).astype(BF16)


def _proj_k_kernel(xn_ref, w_ref, pmat_ref, g_ref, *rest):
    k_ref, kb_ref = rest[-2:]
    acc = jnp.dot(xn_ref[...], w_ref[...], preferred_element_type=F32)
    kn = _head_norm(acc, pmat_ref, g_ref)
    k_ref[...] = kn
    kb_ref[...] = kn.astype(BF16)


def _proj_v_kernel(xn_ref, w_ref, *rest):
    v_ref, vb_ref = rest[-2:]
    acc = jnp.dot(xn_ref[...], w_ref[...], preferred_element_type=F32)
    v_ref[...] = acc
    vb_ref[...] = acc.astype(BF16)


def _proj_u_kernel(xn_ref, w_ref, u_ref):
    u_ref[...] = jnp.dot(xn_ref[...], w_ref[...], preferred_element_type=F32)


def _proj_gz_kernel(xn_ref, wz_ref, wg_ref, o_ref):
    xn = xn_ref[...]
    z = jnp.dot(xn, wz_ref[...], preferred_element_type=F32)
    g = jnp.dot(xn, wg_ref[...], preferred_element_type=F32)
    o_ref[...] = (jax.nn.sigmoid(g) * (z * jax.nn.sigmoid(z))).astype(o_ref.dtype)


def _proj_calls(xn, w_bf, layer, pmat, gq, gk, k_all, v_all, mid):
    r, d = xn.shape
    depth = w_bf.shape[0]
    tm = _tile(r, ROW_TILE)
    nr = r // tm
    x_spec = pl.BlockSpec((tm, d), lambda i: (i, 0))
    full = lambda a: pl.BlockSpec(a.shape, lambda i: (0,) * a.ndim)
    o_spec = pl.BlockSpec((tm, d), lambda i: (i, 0))
    all_spec = pl.BlockSpec((None, tm, d), lambda i: (layer, i, 0))
    all_shape = jax.ShapeDtypeStruct((depth, r, d), F32)
    any_spec = pl.BlockSpec(memory_space=pl.ANY)

    def w_spec(j):
        return pl.BlockSpec((None, d, d), lambda i: (layer, 0, j))

    qn = pl.pallas_call(
        _proj_q_kernel, grid=(nr,),
        in_specs=[x_spec, w_spec(0), full(pmat), full(gq)],
        out_specs=o_spec, out_shape=jax.ShapeDtypeStruct((r, d), BF16),
        compiler_params=_cparams("parallel"), name="proj_q")(xn, w_bf, pmat, gq)
    k_prev = () if k_all is None else (k_all,)
    k_all, kb = pl.pallas_call(
        _proj_k_kernel, grid=(nr,),
        in_specs=[x_spec, w_spec(1), full(pmat), full(gk)] + [any_spec] * len(k_prev),
        out_specs=[all_spec, o_spec],
        out_shape=[all_shape, jax.ShapeDtypeStruct((r, d), BF16)],
        input_output_aliases={4: 0} if k_prev else {},
        compiler_params=_cparams("parallel"), name="proj_k")(xn, w_bf, pmat, gk, *k_prev)
    v_prev = () if v_all is None else (v_all,)
    v_all, vb = pl.pallas_call(
        _proj_v_kernel, grid=(nr,),
        in_specs=[x_spec, w_spec(2)] + [any_spec] * len(v_prev),
        out_specs=[all_spec, o_spec],
        out_shape=[all_shape, jax.ShapeDtypeStruct((r, d), BF16)],
        input_output_aliases={2: 0} if v_prev else {},
        compiler_params=_cparams("parallel"), name="proj_v")(xn, w_bf, *v_prev)
    u = pl.pallas_call(
        _proj_u_kernel, grid=(2, nr),
        in_specs=[pl.BlockSpec((tm, d), lambda j, i: (i, 0)),
                  pl.BlockSpec((None, d, d), lambda j, i: (layer, 0, 3 + j))],
        out_specs=pl.BlockSpec((None, tm, d), lambda j, i: (j, i, 0)),
        out_shape=jax.ShapeDtypeStruct((2, r, d), F32),
        compiler_params=_cparams("parallel", "parallel"), name="proj_u")(xn, w_bf)
    gz = pl.pallas_call(
        _proj_gz_kernel, grid=(N_BRANCH, nr),
        in_specs=[pl.BlockSpec((tm, d), lambda j, i: (i, 0)),
                  pl.BlockSpec((None, d, d), lambda j, i: (layer, 0, 5 + j)),
                  pl.BlockSpec((None, d, d), lambda j, i: (layer, 0, 8 + j))],
        out_specs=pl.BlockSpec((None, tm, d), lambda j, i: (j, i, 0)),
        out_shape=jax.ShapeDtypeStruct((N_BRANCH, r, d), mid),
        compiler_params=_cparams("parallel", "parallel"), name="proj_gz")(xn, w_bf, w_bf)
    return qn, k_all, kb, v_all, vb, u, gz


def _lam_value(lamp_ref, lam_init):
    lp = lamp_ref[...]
    s1 = jnp.sum(lp[0:1] * lp[1:2], axis=1, keepdims=True)
    s2 = jnp.sum(lp[2:3] * lp[3:4], axis=1, keepdims=True)
    return jnp.exp(s1) - jnp.exp(s2) + lam_init


def _bias_by_distance(table, n):
    max_exact = N_BUCKETS // 2
    nf = jnp.maximum(n, 1).astype(F32)
    large = max_exact + (jnp.log(nf / max_exact) / math.log(MAX_DISTANCE / max_exact)
                         * (N_BUCKETS - max_exact)).astype(jnp.int32)
    large = jnp.minimum(large, N_BUCKETS - 1)
    bucket = jnp.where(n < max_exact, n, large)[None]
    col = lambda b: table[b].reshape((-1,) + (1,) * n.ndim)
    out = jnp.broadcast_to(col(0), (table.shape[1],) + n.shape)
    for b in range(1, N_BUCKETS):
        out = jnp.where(bucket == b, col(b), out)
    return out


def _far_distance():
    max_exact = N_BUCKETS // 2
    n = np.arange(1, 4 * MAX_DISTANCE, dtype=np.int64)
    large = max_exact + (np.log(n.astype(np.float32) / np.float32(max_exact))
                         / np.float32(math.log(MAX_DISTANCE / max_exact))
                         * (N_BUCKETS - max_exact)).astype(np.int64)
    not_last = np.nonzero((n < max_exact) | (large < N_BUCKETS - 1))[0]
    return int(n[not_last[-1]] + 1) + 2


def _attn_prompt_kernel(q_ref, k_ref, v_ref, bias_ref, gz_ref, lamp_ref, sg_ref, o_ref,
                        vT_sc, qp_sc, s_sc, m_sc, l_sc, acc_sc, *, t, n_blk, lam_init):
    qi = pl.program_id(2)

    @pl.when(qi == 0)
    def _transpose_values():
        def body(c, carry):
            r0 = pl.multiple_of(c * t, t)
            vT_sc[:, pl.ds(r0, t)] = v_ref[pl.ds(r0, t), :].astype(F32).T.astype(BF16)
            return carry
        lax.fori_loop(0, n_blk, body, 0)

    qT = q_ref[...].astype(F32).T
    row = lax.broadcasted_iota(jnp.int32, qT.shape, 0)
    qp_sc[:, :t] = jnp.where(row < HEAD_DIM, qT, 0.0).astype(BF16)
    qp_sc[:, t:] = jnp.where(row >= HEAD_DIM, qT, 0.0).astype(BF16)
    m_sc[...] = jnp.full(m_sc.shape, NEG_INF, F32)
    l_sc[...] = jnp.zeros(l_sc.shape, F32)
    acc_sc[...] = jnp.zeros(acc_sc.shape, F32)

    def scores(ki, cj, bias_idx):
        r0 = pl.multiple_of(ki * t, t)
        s = jnp.dot(k_ref[pl.ds(r0, t), :], qp_sc[:, cj * t:(cj + 1) * t], preferred_element_type=F32)
        if bias_idx is not None:
            s = s + bias_ref[bias_idx]
        s_sc[cj] = s

    def update(ki, cj):
        r0 = pl.multiple_of(ki * t, t)
        cols = slice(cj * t, (cj + 1) * t)
        s = s_sc[cj]
        m_old = m_sc[:, cols]
        m_new = jnp.maximum(m_old, jnp.max(s, axis=0, keepdims=True))
        alpha = jnp.exp2(m_old - m_new)
        p = jnp.exp2(s - m_new)
        l_sc[:, cols] = alpha * l_sc[:, cols] + jnp.sum(p, axis=0, keepdims=True)
        acc_sc[:, cols] = alpha * acc_sc[:, cols] + jnp.dot(
            vT_sc[:, pl.ds(r0, t)], p.astype(BF16), preferred_element_type=F32)
        m_sc[:, cols] = m_new

    scores(0, 0, None)

    def far_block(ki):
        scores(ki, 1, None)
        update(ki, 0)
        scores(ki + 1, 0, None)
        update(ki, 1)

    def far_pair(j, carry):
        far_block(2 * j)
        far_block(2 * j + 1)
        return carry

    n_far = jnp.maximum(qi - 1, 0)
    lax.fori_loop(0, n_far // 2, far_pair, 0)

    @pl.when(n_far % 2 == 1)
    def _odd_far_block():
        far_block(n_far - 1)

    @pl.when(qi >= 1)
    def _below_diagonal():
        s_sc[0] = s_sc[0] + bias_ref[1]
        scores(qi - 1, 1, 1)
        update(qi - 1, 0)
        scores(qi, 0, 0)
        update(qi - 1, 1)

    @pl.when(qi == 0)
    def _first_block():
        s_sc[0] = s_sc[0] + bias_ref[0]

    scores(qi, 1, 0)
    update(qi, 0)
    update(qi, 1)

    lam = _lam_value(lamp_ref, lam_init)
    on = acc_sc[...] * (1.0 / l_sc[...])
    o = on[:, :t] - lam * on[:, t:]
    ms = jnp.mean(o * o, axis=0, keepdims=True)
    y = o * lax.rsqrt(ms + RMS_EPS) * sg_ref[...] * (1.0 - lam_init)
    o_ref[...] = (y.T * gz_ref[...]).astype(o_ref.dtype)


def _prompt_bias_tiles(rel_bias, t):
    table = (rel_bias - rel_bias[N_BUCKETS - 1][None, :]) * LOG2E
    kk = jnp.arange(t, dtype=jnp.int32)[:, None]
    qq = jnp.arange(t, dtype=jnp.int32)[None, :]
    dist = jnp.stack([jnp.maximum(qq - kk, 0), t + qq - kk])
    tiles = _bias_by_distance(table, dist)
    return jnp.where((jnp.stack([qq - kk, t + qq - kk]) >= 0)[None], tiles, NEG_INF)


def _attn_prompt_call(qn, kb, vb, gz, bias_tiles, lamp, sg_col, lam_init, b, s):
    d = qn.shape[-1]
    t = _tile(s, ATTN_TILE)
    assert t + 1 >= _far_distance(), "key blocks two or more below the diagonal must sit in the last bias bucket"
    q3, k3, v3 = (a.reshape(b, s, d) for a in (qn, kb, vb))
    gz4 = gz.reshape(N_BRANCH, b, s, d)
    out = pl.pallas_call(
        functools.partial(_attn_prompt_kernel, t=t, n_blk=s // t, lam_init=lam_init),
        grid=(b, N_HEADS, s // t),
        in_specs=[
            pl.BlockSpec((None, t, V_DIM), lambda bi, h, qi: (bi, qi, h)),
            pl.BlockSpec((None, s, V_DIM), lambda bi, h, qi: (bi, 0, h)),
            pl.BlockSpec((None, s, V_DIM), lambda bi, h, qi: (bi, 0, h)),
            pl.BlockSpec((None, 2, t, t), lambda bi, h, qi: (h, 0, 0, 0)),
            pl.BlockSpec((None, None, t, V_DIM), lambda bi, h, qi: (0, bi, qi, h)),
            pl.BlockSpec(lamp.shape, lambda bi, h, qi: (0, 0)),
            pl.BlockSpec(sg_col.shape, lambda bi, h, qi: (0, 0)),
        ],
        out_specs=pl.BlockSpec((None, t, V_DIM), lambda bi, h, qi: (bi, qi, h)),
        out_shape=jax.ShapeDtypeStruct((b, s, d), gz.dtype),
        scratch_shapes=[pltpu.VMEM((V_DIM, s), BF16), pltpu.VMEM((V_DIM, 2 * t), BF16),
                        pltpu.VMEM((2, t, t), F32),
                        pltpu.VMEM((1, 2 * t), F32), pltpu.VMEM((1, 2 * t), F32),
                        pltpu.VMEM((V_DIM, 2 * t), F32)],
        compiler_params=_cparams("parallel", "parallel", "arbitrary"),
        name="attn_prompt",
    )(q3, k3, v3, bias_tiles, gz4, lamp, sg_col)
    return out.reshape(b * s, d)


def _attn_sample_kernel(pt_ref, qbd_ref, *refs, pp, page, n_steps, t_new, lam_init):
    kc = refs[:pp]
    vc = refs[pp:2 * pp]
    (kn_ref, vn_ref, bias_ref, biasn_ref, gz_ref, lamp_ref, sg_ref, o_ref,
     m_sc, l_sc, acc_sc, kp_sc, vp_sc) = refs[2 * pp:]
    p = pl.program_id(1)
    grp = 2 * t_new

    @pl.when(p == 0)
    def _init():
        m_sc[...] = jnp.full(m_sc.shape, NEG_INF, F32)
        l_sc[...] = jnp.zeros(l_sc.shape, F32)
        acc_sc[...] = jnp.zeros(acc_sc.shape, F32)

    def attend(k_head, v_head, bias):
        s = jnp.concatenate(
            [lax.dot_general(qbd_ref[h * grp:(h + 1) * grp, :], k_head(h),
                             (((1,), (1,)), ((), ())), preferred_element_type=F32)
             for h in range(N_HEADS)], axis=0) + bias
        m_old = m_sc[...]
        m_new = jnp.maximum(m_old, jnp.max(s, axis=1, keepdims=True))
        alpha = jnp.exp2(m_old - m_new)
        pr = jnp.exp2(s - m_new)
        l_sc[...] = alpha * l_sc[...] + jnp.sum(pr, axis=1, keepdims=True)
        prb = pr.astype(BF16)
        pv = jnp.concatenate(
            [jnp.dot(prb[h * grp:(h + 1) * grp, :], v_head(h), preferred_element_type=F32)
             for h in range(N_HEADS)], axis=0)
        acc_sc[...] = alpha * acc_sc[...] + pv
        m_sc[...] = m_new

    def head_rows(pages, h):
        return jnp.concatenate([r[pl.ds(h, page, stride=N_HEADS), :].astype(BF16) for r in pages], axis=0)

    attend(lambda h: head_rows(kc, h), lambda h: head_rows(vc, h), bias_ref[...])

    @pl.when(p == n_steps - 1)
    def _new_tokens():
        kp_sc[...] = jnp.zeros(kp_sc.shape, F32)
        vp_sc[...] = jnp.zeros(vp_sc.shape, F32)
        kp_sc[0:t_new, :] = kn_ref[...]
        vp_sc[0:t_new, :] = vn_ref[...]
        attend(lambda h: kp_sc[:, h * V_DIM:(h + 1) * V_DIM].astype(BF16),
               lambda h: vp_sc[:, h * V_DIM:(h + 1) * V_DIM].astype(BF16), biasn_ref[...])
        lam = _lam_value(lamp_ref, lam_init)
        on = acc_sc[...] * (1.0 / l_sc[...])
        for h in range(N_HEADS):
            r1 = h * grp
            o = on[r1:r1 + t_new, :] - lam * on[r1 + t_new:r1 + grp, :]
            ms = jnp.mean(o * o, axis=1, keepdims=True)
            y = o * lax.rsqrt(ms + RMS_EPS) * sg_ref[...] * (1.0 - lam_init)
            lanes = slice(h * V_DIM, (h + 1) * V_DIM)
            o_ref[:, lanes] = y * gz_ref[:, lanes]


def _sample_bias(rel_bias, past, page, t_new):
    j = jnp.arange(past + page, dtype=jnp.int32)[None, :]
    tt = jnp.arange(t_new, dtype=jnp.int32)[:, None]
    dist = past + tt - j
    vals = _bias_by_distance(rel_bias * LOG2E, jnp.maximum(dist, 0))
    vals = jnp.where((dist >= 0)[None], vals, NEG_INF)
    vals = jnp.broadcast_to(vals[:, None], (N_HEADS, 2, t_new, past + page))
    return vals.reshape(N_HEADS * 2 * t_new, past + page)


def _attn_sample_call(qn, kn, v, gz, cache_k4, cache_v4, page_flat, layer, bias, lamp, sg_row,
                      lam_init, db, t_new, n_pages, page):
    d = qn.shape[-1]
    rows = 2 * N_HEADS * t_new
    pp = SAMPLE_PAGES if n_pages % SAMPLE_PAGES == 0 else 1
    n_steps = n_pages // pp
    past = n_pages * page
    q5 = jnp.transpose(qn.reshape(db, t_new, N_HEADS, 2, HEAD_DIM), (0, 2, 3, 1, 4))
    eye = jnp.eye(2, dtype=BF16)
    qbd = (q5[:, :, :, :, None, :] * eye[None, None, :, None, :, None]).reshape(db, rows, V_DIM)
    kn4 = kn.reshape(-1, db, t_new, d)
    v4 = v.reshape(-1, db, t_new, d)
    gz4 = gz.reshape(N_BRANCH, db, t_new, d)

    def page_spec(j):
        return pl.BlockSpec((None, None, page * N_HEADS, V_DIM),
                            lambda b, p, pt, j=j: (layer, pt[b * n_pages + p * pp + j], 0, 0))

    tok_spec = pl.BlockSpec((None, None, t_new, d), lambda b, p, pt: (layer, b, 0, 0))
    grid_spec = pltpu.PrefetchScalarGridSpec(
        num_scalar_prefetch=1,
        grid=(db, n_steps),
        in_specs=[pl.BlockSpec((None, rows, V_DIM), lambda b, p, pt: (b, 0, 0))]
        + [page_spec(j) for j in range(pp)] + [page_spec(j) for j in range(pp)]
        + [tok_spec, tok_spec,
           pl.BlockSpec((rows, pp * page), lambda b, p, pt: (0, p)),
           pl.BlockSpec((rows, page), lambda b, p, pt: (0, past // page)),
           pl.BlockSpec((None, None, t_new, d), lambda b, p, pt: (0, b, 0, 0)),
           pl.BlockSpec(lamp.shape, lambda b, p, pt: (0, 0)),
           pl.BlockSpec(sg_row.shape, lambda b, p, pt: (0, 0))],
        out_specs=pl.BlockSpec((None, t_new, d), lambda b, p, pt: (b, 0, 0)),
        scratch_shapes=[pltpu.VMEM((rows, 1), F32), pltpu.VMEM((rows, 1), F32),
                        pltpu.VMEM((rows, V_DIM), F32), pltpu.VMEM((page, d), F32),
                        pltpu.VMEM((page, d), F32)],
    )
    out = pl.pallas_call(
        functools.partial(_attn_sample_kernel, pp=pp, page=page, n_steps=n_steps, t_new=t_new,
                          lam_init=lam_init),
        grid_spec=grid_spec,
        out_shape=jax.ShapeDtypeStruct((db, t_new, d), F32),
        compiler_params=_cparams("parallel", "arbitrary"),
        name="attn_sample",
    )(page_flat, qbd, *([cache_k4] * pp), *([cache_v4] * pp), kn4, v4, bias, bias, gz4, lamp, sg_row)
    return out.reshape(db * t_new, d)


def _pool_kernel(u_ref, hist_ref, gz_ref, w_ref, scale_ref, o_ref, ext_sc, *, tm, pos0, carry):
    c = pl.program_id(1)
    nb = u_ref.shape[0]
    grp = u_ref.shape[2] // len(POOL_WINDOWS)

    @pl.when(c == 0)
    def _hist():
        ext_sc[:, 0:POOL_HIST, :] = hist_ref[...]

    ext_sc[:, POOL_HIST:POOL_HIST + tm, :] = u_ref[...]
    pos = pos0 + c * tm + lax.broadcasted_iota(jnp.int32, (nb, tm, grp), 1)
    for gi, win in enumerate(POOL_WINDOWS):
        lanes = slice(gi * grp, (gi + 1) * grp)
        tot = ext_sc[:, POOL_HIST:POOL_HIST + tm, lanes]
        for j in range(1, win):
            tot = tot + ext_sc[:, POOL_HIST - j:POOL_HIST - j + tm, lanes]
        cnt = jnp.minimum(pos + 1, win).astype(F32)
        pooled = tot / cnt - u_ref[:, :, lanes]
        mixed = jnp.dot(pooled.reshape(nb * tm, grp).astype(BF16), w_ref[gi],
                        preferred_element_type=F32).reshape(nb, tm, grp)
        o_ref[:, :, lanes] = (mixed * scale_ref[:, lanes] * gz_ref[:, :, lanes]).astype(o_ref.dtype)
    if carry:
        ext_sc[:, 0:POOL_HIST, :] = ext_sc[:, tm:tm + POOL_HIST, :]


def _pool_call(u4, hist, gz4, pool_w_bf, pool_scale, pos0, nb):
    _, n, t, d = u4.shape
    tm = _tile(t, SEQ_TILE)
    nc = t // tm
    grp = d // len(POOL_WINDOWS)
    blk = lambda i, c: (i, c, 0)
    return pl.pallas_call(
        functools.partial(_pool_kernel, tm=tm, pos0=pos0, carry=nc > 1),
        grid=(n // nb, nc),
        in_specs=[pl.BlockSpec((None, nb, tm, d), lambda i, c: (0, i, c, 0)),
                  pl.BlockSpec((nb, POOL_HIST, d), lambda i, c: (i, 0, 0)),
                  pl.BlockSpec((None, nb, tm, d), lambda i, c: (1, i, c, 0)),
                  pl.BlockSpec((len(POOL_WINDOWS), grp, grp), lambda i, c: (0, 0, 0)),
                  pl.BlockSpec((1, d), lambda i, c: (0, 0))],
        out_specs=pl.BlockSpec((nb, tm, d), blk),
        out_shape=jax.ShapeDtypeStruct((n, t, d), gz4.dtype),
        scratch_shapes=[pltpu.VMEM((nb, tm + POOL_HIST, d), F32)],
        compiler_params=_cparams("parallel", "arbitrary"),
        name="pool_mix",
    )(u4, hist, gz4, pool_w_bf, pool_scale)


def _lru_kernel(u_ref, hist_ref, h0_ref, gz_ref, cw_ref, cb_ref, wa_ref, wx_ref, ba_ref, bx_ref,
                lam_ref, o_ref, hn_ref, ext_sc, h_sc, *, tm, carry):
    c = pl.program_id(1)
    nb, _, d = u_ref.shape

    @pl.when(c == 0)
    def _state():
        ext_sc[:, 0:CONV_HIST, :] = hist_ref[...]
        h_sc[...] = h0_ref[...]

    ext_sc[:, CONV_HIST:CONV_HIST + tm, :] = u_ref[...]
    xc = cb_ref[...] + ext_sc[:, CONV_HIST - (CONV_WIDTH - 1):CONV_HIST - (CONV_WIDTH - 1) + tm, :] * cw_ref[0:1, :]
    for j in range(1, CONV_WIDTH):
        lo = CONV_HIST - (CONV_WIDTH - 1) + j
        xc = xc + ext_sc[:, lo:lo + tm, :] * cw_ref[j:j + 1, :]
    xb = xc.reshape(nb * tm, d).astype(BF16)
    nt = d // GATE_TILE
    ra = jnp.concatenate([jnp.dot(xb[:, j * GATE_TILE:(j + 1) * GATE_TILE], wa_ref[j],
                                  preferred_element_type=F32) for j in range(nt)], axis=1)
    rx = jnp.concatenate([jnp.dot(xb[:, j * GATE_TILE:(j + 1) * GATE_TILE], wx_ref[j],
                                  preferred_element_type=F32) for j in range(nt)], axis=1)
    r = jax.nn.sigmoid(ra + ba_ref[...]).reshape(nb, tm, d)
    i = jax.nn.sigmoid(rx + bx_ref[...]).reshape(nb, tm, d)
    log_a = (-LRU_C) * r * jax.nn.softplus(-lam_ref[...])
    a = jnp.exp(log_a)
    bt = jnp.sqrt(-jnp.tanh(log_a) * (a * a + 1.0)) * (i * xc)
    row = lax.broadcasted_iota(jnp.int32, (nb, tm, d), 1) % SCAN_GROUP
    sft = 1
    while sft < SCAN_GROUP:
        keep = row >= sft
        a_prev = jnp.where(keep, pltpu.roll(a, sft, axis=1), 1.0)
        b_prev = jnp.where(keep, pltpu.roll(bt, sft, axis=1), 0.0)
        bt = a * b_prev + bt
        a = a * a_prev
        sft *= 2
    h_prev = h_sc[...]
    groups = []
    for g in range(tm // SCAN_GROUP):
        rows = slice(g * SCAN_GROUP, (g + 1) * SCAN_GROUP)
        hg = bt[:, rows, :] + a[:, rows, :] * h_prev
        groups.append(hg)
        h_prev = hg[:, SCAN_GROUP - 1:SCAN_GROUP, :]
    h = jnp.concatenate(groups, axis=1)
    h_last = h_prev
    h_sc[...] = h_last
    hn_ref[...] = h_last
    o_ref[...] = (h * gz_ref[...]).astype(o_ref.dtype)
    if carry:
        ext_sc[:, 0:CONV_HIST, :] = ext_sc[:, tm:tm + CONV_HIST, :]


def _lru_call(u4, hist, h0, gz4, conv_w, conv_b, wa, wx, ba, bx, lam, nb):
    _, n, t, d = u4.shape
    tm = _tile(t, SEQ_TILE)
    nc = t // tm
    blk = lambda i, c: (i, c, 0)
    full2 = lambda a: pl.BlockSpec(a.shape, lambda i, c: (0,) * a.ndim)
    return pl.pallas_call(
        functools.partial(_lru_kernel, tm=tm, carry=nc > 1),
        grid=(n // nb, nc),
        in_specs=[pl.BlockSpec((None, nb, tm, d), lambda i, c: (1, i, c, 0)),
                  pl.BlockSpec((nb, CONV_HIST, d), lambda i, c: (i, 0, 0)),
                  pl.BlockSpec((nb, 1, d), lambda i, c: (i, 0, 0)),
                  pl.BlockSpec((None, nb, tm, d), lambda i, c: (2, i, c, 0)),
                  full2(conv_w), full2(conv_b), full2(wa), full2(wx), full2(ba), full2(bx), full2(lam)],
        out_specs=[pl.BlockSpec((nb, tm, d), blk), pl.BlockSpec((nb, 1, d), lambda i, c: (i, 0, 0))],
        out_shape=[jax.ShapeDtypeStruct((n, t, d), gz4.dtype), jax.ShapeDtypeStruct((n, 1, d), F32)],
        scratch_shapes=[pltpu.VMEM((nb, tm + CONV_HIST, d), F32), pltpu.VMEM((nb, 1, d), F32)],
        compiler_params=_cparams("parallel", "arbitrary"),
        name="conv_rglru",
    )(u4, hist, h0, gz4, conv_w, conv_b, wa, wx, ba, bx, lam)


def _out_kernel(x_ref, ma_ref, mp_ref, ml_ref, w_ref, g_ref, y_ref, xn_ref):
    m = (ma_ref[...].astype(F32) + mp_ref[...].astype(F32) + ml_ref[...].astype(F32)).astype(BF16)
    y = x_ref[...] + jnp.dot(m, w_ref[...], preferred_element_type=F32)
    y_ref[...] = y
    xn_ref[...] = _rms_rows(y, g_ref[...]).astype(BF16)


def _out_call(x, ma, mp, ml, w_out_bf, layer, g_next):
    r, d = x.shape
    tm = _tile(r, ROW_TILE // 2)
    spec = pl.BlockSpec((tm, d), lambda i: (i, 0))
    return pl.pallas_call(
        _out_kernel, grid=(r // tm,),
        in_specs=[spec, spec, spec, spec,
                  pl.BlockSpec((None, d, d), lambda i: (layer, 0, 0)),
                  pl.BlockSpec((1, d), lambda i: (0, 0))],
        out_specs=[spec, spec],
        out_shape=[jax.ShapeDtypeStruct((r, d), F32), jax.ShapeDtypeStruct((r, d), BF16)],
        compiler_params=_cparams("parallel"),
        name="out_proj",
    )(x, ma, mp, ml, w_out_bf, g_next)


def _block_diag(w, tile):
    n, c, _ = w.shape
    per = tile // c
    w4 = w.reshape(n // per, per, c, c)
    eye = jnp.eye(per, dtype=w.dtype)
    return (w4[:, :, :, None, :] * eye[None, :, None, :, None]).reshape(n // per, tile, tile)


def _mixers_and_out(x, xn, lw, layer, attn_fn, nseq, t, pos0, pool_hist, conv_hist, h0, nb, g_next, k_all, v_all,
                    mid):
    d = x.shape[-1]
    qn, k_all, kb, v_all, vb, u, gz = _proj_calls(xn, lw["w_in"], layer, lw["pmat"], lw["gq"], lw["gk"],
                                                  k_all, v_all, mid)
    m_attn = attn_fn(qn, k_all, kb, v_all, vb, gz)
    u4 = u.reshape(2, nseq, t, d)
    gz4 = gz.reshape(N_BRANCH, nseq, t, d)
    m_pool = _pool_call(u4, pool_hist, gz4, lw["pool_w"], lw["pool_scale"], pos0, nb)
    m_lru, h_new = _lru_call(u4, conv_hist, h0, gz4, lw["conv_w"], lw["conv_b"],
                             lw["wa"], lw["wx"], lw["ba"], lw["bx"], lw["lam"], nb)
    y, xn_next = _out_call(x, m_attn, m_pool.reshape(nseq * t, d), m_lru.reshape(nseq * t, d),
                           lw["w_out"], layer, g_next)
    return y, xn_next, k_all, v_all, u4, h_new.reshape(nseq, d)


def kernel(x_prompt, x_sample, cache_k, cache_v, page_table, state_pool, state_conv, state_h, rel_bias, norm_g, w_in, q_norm_g, k_norm_g, lam_q1, lam_k1, lam_q2, lam_k2, subln_g, pool_w, pool_scale, conv_w, conv_b, gate_a_w, gate_a_b, gate_x_w, gate_x_b, lru_lambda, w_out):
    b, s, d = x_prompt.shape
    db, t_new, _ = x_sample.shape
    depth = w_in.shape[0]
    n_phys, page = cache_k.shape[1], cache_k.shape[2]
    n_pages = page_table.shape[1]
    past = n_pages * page
    assert d == N_HEADS * V_DIM and d % GATE_TILE == 0 and GATE_TILE % (d // LRU_BLOCKS) == 0
    assert s >= POOL_BUF and cache_k.shape[3:] == (N_HEADS, V_DIM)

    w_in_bf = w_in.astype(BF16)
    w_out_bf = w_out.astype(BF16)
    pool_w_bf = pool_w.astype(BF16)
    pmat = jnp.kron(jnp.eye(d // HEAD_DIM, dtype=F32), jnp.ones((HEAD_DIM, HEAD_DIM), F32)).astype(BF16)
    reps = d // HEAD_DIM
    cache_k4 = cache_k.reshape(depth, n_phys, page * N_HEADS, V_DIM)
    cache_v4 = cache_v.reshape(depth, n_phys, page * N_HEADS, V_DIM)
    page_flat = page_table.reshape(-1).astype(jnp.int32)

    t_attn = _tile(s, ATTN_TILE)
    bias_tiles = _prompt_bias_tiles(rel_bias, t_attn)
    bias_s = _sample_bias(rel_bias, past, page, t_new)

    pool0 = jnp.zeros((b, POOL_HIST, d), F32)
    conv0 = jnp.zeros((b, CONV_HIST, d), F32)
    h00 = jnp.zeros((b, 1, d), F32)
    nb_s = 16 if db % 16 == 0 else 1

    xp = x_prompt.reshape(b * s, d)
    xs = x_sample.reshape(db * t_new, d)
    xnp = _norm_call(xp, norm_g[0][None])
    xns = _norm_call(xs, norm_g[0][None])

    kp = vp = ksm = vsm = None
    outs = [[] for _ in range(6)]
    for l in range(depth):
        lam_init = 0.8 - 0.6 * math.exp(-0.3 * l)
        lw = {
            "w_in": w_in_bf, "w_out": w_out_bf, "pmat": pmat,
            "gq": jnp.tile(q_norm_g[l] * (HEAD_DIM ** -0.5 * LOG2E), reps)[None],
            "gk": jnp.tile(k_norm_g[l], reps)[None],
            "pool_w": pool_w_bf[l], "pool_scale": pool_scale[l][None],
            "conv_w": conv_w[l], "conv_b": conv_b[l][None],
            "wa": _block_diag(gate_a_w[l], GATE_TILE).astype(BF16),
            "wx": _block_diag(gate_x_w[l], GATE_TILE).astype(BF16),
            "ba": gate_a_b[l].reshape(1, d), "bx": gate_x_b[l].reshape(1, d),
            "lam": lru_lambda[l][None],
        }
        lamp = jnp.stack([lam_q1[l], lam_k1[l], lam_q2[l], lam_k2[l]])
        g_next = norm_g[(l + 1) % depth][None]

        attn_p = lambda qn, kn, kb, v, vb, gz: _attn_prompt_call(
            qn, kb, vb, gz, bias_tiles, lamp, subln_g[l][:, None], lam_init, b, s)
        xp, xnp, kp, vp, up4, hp = _mixers_and_out(
            xp, xnp, lw, l, attn_p, b, s, 0, pool0, conv0, h00, 1, g_next, kp, vp, BF16)

        attn_s = lambda qn, kn, kb, v, vb, gz: _attn_sample_call(
            qn, kn, v, gz, cache_k4, cache_v4, page_flat, l, bias_s, lamp, subln_g[l][None],
            lam_init, db, t_new, n_pages, page)
        pool_hist = jnp.concatenate([jnp.zeros((db, POOL_HIST - POOL_BUF, d), F32), state_pool[l]], axis=1)
        conv_hist = jnp.concatenate([jnp.zeros((db, CONV_HIST - (CONV_WIDTH - 1), d), F32), state_conv[l]],
                                    axis=1)
        xs, xns, ksm, vsm, us4, hs = _mixers_and_out(
            xs, xns, lw, l, attn_s, db, t_new, past, pool_hist, conv_hist, state_h[l][:, None], nb_s, g_next,
            ksm, vsm, F32)

        new = (up4[0, :, -POOL_BUF:],
               jnp.concatenate([state_pool[l], us4[0]], axis=1)[:, -POOL_BUF:],
               up4[1, :, -(CONV_WIDTH - 1):],
               jnp.concatenate([state_conv[l], us4[1]], axis=1)[:, -(CONV_WIDTH - 1):],
               hp, hs)
        for acc, val in zip(outs, new):
            acc.append(val)
    return ((xp.reshape(b, s, d), xs.reshape(db, t_new, d),
             kp.reshape(depth, b, s, N_HEADS, V_DIM), vp.reshape(depth, b, s, N_HEADS, V_DIM),
             ksm.reshape(depth, db, t_new, N_HEADS, V_DIM), vsm.reshape(depth, db, t_new, N_HEADS, V_DIM))
            + tuple(jnp.stack(o) for o in outs))
```

```python
import functools
import math

import numpy as np
import jax
import jax.numpy as jnp
from jax import lax
from jax.experimental import pallas as pl
from jax.experimental.pallas import tpu as pltpu

F32 = jnp.float32
BF16 = jnp.bfloat16

RMS_EPS = 1e-6
NEG_INF = -1e30
LOG2E = 1.4426950408889634

N_HEADS = 8
HEAD_DIM = 64
V_DIM = 2 * HEAD_DIM
N_BUCKETS = 32
MAX_DISTANCE = 128
POOL_WINDOWS = (2, 4, 8, 16)
POOL_BUF = max(POOL_WINDOWS) - 1
POOL_HIST = 16
CONV_WIDTH = 4
CONV_HIST = 8
LRU_BLOCKS = 16
LRU_C = 8.0
N_BRANCH = 3
GATE_TILE = 256

ROW_TILE = 1024
ATTN_TILE = 512
SEQ_TILE = 256
SCAN_GROUP = 8
SAMPLE_PAGES = 4
VMEM_LIMIT = 56 * 1024 * 1024


def _cparams(*sem):
    return pltpu.CompilerParams(dimension_semantics=sem, vmem_limit_bytes=VMEM_LIMIT)


def _tile(n, pref):
    return pref if n % pref == 0 else n


def _layer_spec(a, layer):
    zeros = (0,) * (a.ndim - 1)
    return pl.BlockSpec((None,) + a.shape[1:], lambda *_: (layer,) + zeros)


def _rms_rows(x, g):
    ms = jnp.mean(x * x, axis=-1, keepdims=True)
    return x * lax.rsqrt(ms + RMS_EPS) * g


def _norm_kernel(x_ref, g_ref, o_ref):
    o_ref[...] = _rms_rows(x_ref[...], g_ref[...]).astype(BF16)


def _norm_call(x, g):
    r, d = x.shape
    tm = _tile(r, ROW_TILE)
    return pl.pallas_call(
        _norm_kernel,
        grid=(r // tm,),
        in_specs=[pl.BlockSpec((tm, d), lambda i: (i, 0)), _layer_spec(g, 0)],
        out_specs=pl.BlockSpec((tm, d), lambda i: (i, 0)),
        out_shape=jax.ShapeDtypeStruct((r, d), BF16),
        compiler_params=_cparams("parallel"),
        name="prenorm",
    )(x, g)


def _head_norm(acc, pmat_ref, g_ref):
    ss = jnp.dot((acc * acc).astype(BF16), pmat_ref[...], preferred_element_type=F32)
    return acc * lax.rsqrt(ss * (1.0 / HEAD_DIM) + RMS_EPS) * g_ref[...]


def _proj_q_kernel(xn_ref, w_ref, pmat_ref, g_ref, q_ref):
    acc = jnp.dot(xn_ref[...], w_ref[...], preferred_element_type=F32)
    q_ref[...] = _head_norm(acc, pmat_ref, g_ref).astype(BF16)


def _proj_k_kernel(xn_ref, w_ref, pmat_ref, g_ref, *rest):
    k_ref, kb_ref = rest[-2:]
    acc = jnp.dot(xn_ref[...], w_ref[...], preferred_element_type=F32)
    kn = _head_norm(acc, pmat_ref, g_ref)
    k_ref[...] = kn
    kb_ref[...] = kn.astype(BF16)


def _proj_v_kernel(xn_ref, w_ref, *rest):
    v_ref, vb_ref = rest[-2:]
    acc = jnp.dot(xn_ref[...], w_ref[...], preferred_element_type=F32)
    v_ref[...] = acc
    vb_ref[...] = acc.astype(BF16)


def _proj_u_kernel(xn_ref, w_ref, u_ref):
    u_ref[...] = jnp.dot(xn_ref[...], w_ref[...], preferred_element_type=F32)


def _proj_gz_kernel(xn_ref, wz_ref, wg_ref, o_ref):
    xn = xn_ref[...]
    z = jnp.dot(xn, wz_ref[...], preferred_element_type=F32)
    g = jnp.dot(xn, wg_ref[...], preferred_element_type=F32)
    o_ref[...] = (jax.nn.sigmoid(g) * (z * jax.nn.sigmoid(z))).astype(o_ref.dtype)


def _proj_calls(xn, w_bf, layer, pmat, gq, gk, k_all, v_all, mid):
    r, d = xn.shape
    depth = w_bf.shape[0]
    tm = _tile(r, ROW_TILE)
    nr = r // tm
    x_spec = pl.BlockSpec((tm, d), lambda i: (i, 0))
    full = lambda a: pl.BlockSpec(a.shape, lambda i: (0,) * a.ndim)
    o_spec = pl.BlockSpec((tm, d), lambda i: (i, 0))
    all_spec = pl.BlockSpec((None, tm, d), lambda i: (layer, i, 0))
    all_shape = jax.ShapeDtypeStruct((depth, r, d), F32)
    any_spec = pl.BlockSpec(memory_space=pl.ANY)

    def w_spec(j):
        return pl.BlockSpec((None, d, d), lambda i: (layer, 0, j))

    qn = pl.pallas_call(
        _proj_q_kernel, grid=(nr,),
        in_specs=[x_spec, w_spec(0), full(pmat), _layer_spec(gq, layer)],
        out_specs=o_spec, out_shape=jax.ShapeDtypeStruct((r, d), BF16),
        compiler_params=_cparams("parallel"), name="proj_q")(xn, w_bf, pmat, gq)
    k_prev = () if k_all is None else (k_all,)
    k_all, kb = pl.pallas_call(
        _proj_k_kernel, grid=(nr,),
        in_specs=[x_spec, w_spec(1), full(pmat), _layer_spec(gk, layer)] + [any_spec] * len(k_prev),
        out_specs=[all_spec, o_spec],
        out_shape=[all_shape, jax.ShapeDtypeStruct((r, d), BF16)],
        input_output_aliases={4: 0} if k_prev else {},
        compiler_params=_cparams("parallel"), name="proj_k")(xn, w_bf, pmat, gk, *k_prev)
    v_prev = () if v_all is None else (v_all,)
    v_all, vb = pl.pallas_call(
        _proj_v_kernel, grid=(nr,),
        in_specs=[x_spec, w_spec(2)] + [any_spec] * len(v_prev),
        out_specs=[all_spec, o_spec],
        out_shape=[all_shape, jax.ShapeDtypeStruct((r, d), BF16)],
        input_output_aliases={2: 0} if v_prev else {},
        compiler_params=_cparams("parallel"), name="proj_v")(xn, w_bf, *v_prev)
    u = pl.pallas_call(
        _proj_u_kernel, grid=(2, nr),
        in_specs=[pl.BlockSpec((tm, d), lambda j, i: (i, 0)),
                  pl.BlockSpec((None, d, d), lambda j, i: (layer, 0, 3 + j))],
        out_specs=pl.BlockSpec((None, tm, d), lambda j, i: (j, i, 0)),
        out_shape=jax.ShapeDtypeStruct((2, r, d), F32),
        compiler_params=_cparams("parallel", "parallel"), name="proj_u")(xn, w_bf)
    gz = pl.pallas_call(
        _proj_gz_kernel, grid=(N_BRANCH, nr),
        in_specs=[pl.BlockSpec((tm, d), lambda j, i: (i, 0)),
                  pl.BlockSpec((None, d, d), lambda j, i: (layer, 0, 5 + j)),
                  pl.BlockSpec((None, d, d), lambda j, i: (layer, 0, 8 + j))],
        out_specs=pl.BlockSpec((None, tm, d), lambda j, i: (j, i, 0)),
        out_shape=jax.ShapeDtypeStruct((N_BRANCH, r, d), mid),
        compiler_params=_cparams("parallel", "parallel"), name="proj_gz")(xn, w_bf, w_bf)
    return qn, k_all, kb, v_all, vb, u, gz


def _lam_value(lamp_ref, lam_init):
    lp = lamp_ref[...]
    s1 = jnp.sum(lp[0:1] * lp[1:2], axis=1, keepdims=True)
    s2 = jnp.sum(lp[2:3] * lp[3:4], axis=1, keepdims=True)
    return jnp.exp(s1) - jnp.exp(s2) + lam_init


def _bias_by_distance(table, n):
    max_exact = N_BUCKETS // 2
    nf = jnp.maximum(n, 1).astype(F32)
    large = max_exact + (jnp.log(nf / max_exact) / math.log(MAX_DISTANCE / max_exact)
                         * (N_BUCKETS - max_exact)).astype(jnp.int32)
    large = jnp.minimum(large, N_BUCKETS - 1)
    bucket = jnp.where(n < max_exact, n, large)[None]
    col = lambda b: table[b].reshape((-1,) + (1,) * n.ndim)
    out = jnp.broadcast_to(col(0), (table.shape[1],) + n.shape)
    for b in range(1, N_BUCKETS):
        out = jnp.where(bucket == b, col(b), out)
    return out


def _far_distance():
    max_exact = N_BUCKETS // 2
    n = np.arange(1, 4 * MAX_DISTANCE, dtype=np.int64)
    large = max_exact + (np.log(n.astype(np.float32) / np.float32(max_exact))
                         / np.float32(math.log(MAX_DISTANCE / max_exact))
                         * (N_BUCKETS - max_exact)).astype(np.int64)
    not_last = np.nonzero((n < max_exact) | (large < N_BUCKETS - 1))[0]
    return int(n[not_last[-1]] + 1) + 2


def _attn_prompt_kernel(q_ref, k_ref, v_ref, bias_ref, gz_ref, lamp_ref, sg_ref, o_ref,
                        vT_sc, qp_sc, s_sc, m_sc, l_sc, acc_sc, *, t, n_blk, lam_init):
    qi = pl.program_id(2)

    @pl.when(qi == 0)
    def _transpose_values():
        def body(c, carry):
            r0 = pl.multiple_of(c * t, t)
            vT_sc[:, pl.ds(r0, t)] = v_ref[pl.ds(r0, t), :].astype(F32).T.astype(BF16)
            return carry
        lax.fori_loop(0, n_blk, body, 0)

    qT = q_ref[...].astype(F32).T
    row = lax.broadcasted_iota(jnp.int32, qT.shape, 0)
    qp_sc[:, :t] = jnp.where(row < HEAD_DIM, qT, 0.0).astype(BF16)
    qp_sc[:, t:] = jnp.where(row >= HEAD_DIM, qT, 0.0).astype(BF16)
    m_sc[...] = jnp.full(m_sc.shape, NEG_INF, F32)
    l_sc[...] = jnp.zeros(l_sc.shape, F32)
    acc_sc[...] = jnp.zeros(acc_sc.shape, F32)

    def scores(ki, cj, bias_idx):
        r0 = pl.multiple_of(ki * t, t)
        s = jnp.dot(k_ref[pl.ds(r0, t), :], qp_sc[:, cj * t:(cj + 1) * t], preferred_element_type=F32)
        if bias_idx is not None:
            s = s + bias_ref[bias_idx]
        s_sc[cj] = s

    def update(ki, cj):
        r0 = pl.multiple_of(ki * t, t)
        cols = slice(cj * t, (cj + 1) * t)
        s = s_sc[cj]
        m_old = m_sc[:, cols]
        m_new = jnp.maximum(m_old, jnp.max(s, axis=0, keepdims=True))
        alpha = jnp.exp2(m_old - m_new)
        p = jnp.exp2(s - m_new)
        l_sc[:, cols] = alpha * l_sc[:, cols] + jnp.sum(p, axis=0, keepdims=True)
        acc_sc[:, cols] = alpha * acc_sc[:, cols] + jnp.dot(
            vT_sc[:, pl.ds(r0, t)], p.astype(BF16), preferred_element_type=F32)
        m_sc[:, cols] = m_new

    scores(0, 0, None)

    def far_block(ki):
        scores(ki, 1, None)
        update(ki, 0)
        scores(ki + 1, 0, None)
        update(ki, 1)

    def far_pair(j, carry):
        far_block(2 * j)
        far_block(2 * j + 1)
        return carry

    n_far = jnp.maximum(qi - 1, 0)
    lax.fori_loop(0, n_far // 2, far_pair, 0)

    @pl.when(n_far % 2 == 1)
    def _odd_far_block():
        far_block(n_far - 1)

    @pl.when(qi >= 1)
    def _below_diagonal():
        s_sc[0] = s_sc[0] + bias_ref[1]
        scores(qi - 1, 1, 1)
        update(qi - 1, 0)
        scores(qi, 0, 0)
        update(qi - 1, 1)

    @pl.when(qi == 0)
    def _first_block():
        s_sc[0] = s_sc[0] + bias_ref[0]

    scores(qi, 1, 0)
    update(qi, 0)
    update(qi, 1)

    lam = _lam_value(lamp_ref, lam_init)
    on = acc_sc[...] * (1.0 / l_sc[...])
    o = on[:, :t] - lam * on[:, t:]
    ms = jnp.mean(o * o, axis=0, keepdims=True)
    y = o * lax.rsqrt(ms + RMS_EPS) * sg_ref[...] * (1.0 - lam_init)
    o_ref[...] = (y.T * gz_ref[...]).astype(o_ref.dtype)


def _prompt_bias_tiles(rel_bias, t):
    table = (rel_bias - rel_bias[N_BUCKETS - 1][None, :]) * LOG2E
    kk = jnp.arange(t, dtype=jnp.int32)[:, None]
    qq = jnp.arange(t, dtype=jnp.int32)[None, :]
    dist = jnp.stack([jnp.maximum(qq - kk, 0), t + qq - kk])
    tiles = _bias_by_distance(table, dist)
    return jnp.where((jnp.stack([qq - kk, t + qq - kk]) >= 0)[None], tiles, NEG_INF)


def _attn_prompt_call(qn, kb, vb, gz, bias_tiles, layer, lamp, sg_col, lam_init, b, s):
    d = qn.shape[-1]
    t = _tile(s, ATTN_TILE)
    assert t + 1 >= _far_distance(), "key blocks two or more below the diagonal must sit in the last bias bucket"
    q3, k3, v3 = (a.reshape(b, s, d) for a in (qn, kb, vb))
    gz4 = gz.reshape(N_BRANCH, b, s, d)
    out = pl.pallas_call(
        functools.partial(_attn_prompt_kernel, t=t, n_blk=s // t, lam_init=lam_init),
        grid=(b, N_HEADS, s // t),
        in_specs=[
            pl.BlockSpec((None, t, V_DIM), lambda bi, h, qi: (bi, qi, h)),
            pl.BlockSpec((None, s, V_DIM), lambda bi, h, qi: (bi, 0, h)),
            pl.BlockSpec((None, s, V_DIM), lambda bi, h, qi: (bi, 0, h)),
            pl.BlockSpec((None, 2, t, t), lambda bi, h, qi: (h, 0, 0, 0)),
            pl.BlockSpec((None, None, t, V_DIM), lambda bi, h, qi: (0, bi, qi, h)),
            _layer_spec(lamp, layer),
            _layer_spec(sg_col, layer),
        ],
        out_specs=pl.BlockSpec((None, t, V_DIM), lambda bi, h, qi: (bi, qi, h)),
        out_shape=jax.ShapeDtypeStruct((b, s, d), gz.dtype),
        scratch_shapes=[pltpu.VMEM((V_DIM, s), BF16), pltpu.VMEM((V_DIM, 2 * t), BF16),
                        pltpu.VMEM((2, t, t), F32),
                        pltpu.VMEM((1, 2 * t), F32), pltpu.VMEM((1, 2 * t), F32),
                        pltpu.VMEM((V_DIM, 2 * t), F32)],
        compiler_params=_cparams("parallel", "parallel", "arbitrary"),
        name="attn_prompt",
    )(q3, k3, v3, bias_tiles, gz4, lamp, sg_col)
    return out.reshape(b * s, d)


def _attn_sample_kernel(pt_ref, qbdT_ref, *refs, pp, n_steps, t_new, lam_init):
    kc = refs[:pp]
    vc = refs[pp:2 * pp]
    (kn_ref, vn_ref, mask_ref, maskn_ref, gz_ref, lamp_ref, sg_ref, o_ref, m_sc, l_sc, acc_sc) = refs[2 * pp:]
    p = pl.program_id(1)
    grp = 2 * t_new
    over_rows = (((0,), (0,)), ((), ()))

    @pl.when(p == 0)
    def _init():
        m_sc[...] = jnp.full(m_sc.shape, NEG_INF, F32)
        l_sc[...] = jnp.zeros(l_sc.shape, F32)
        acc_sc[...] = jnp.zeros(acc_sc.shape, F32)

    def attend(k_rows, v_rows, masks):
        s = [jnp.dot(k.astype(BF16), qbdT_ref[...], preferred_element_type=F32) + mk
             for k, mk in zip(k_rows, masks)]
        m_old = m_sc[...]
        m_new = m_old
        for sj in s:
            m_new = jnp.maximum(m_new, jnp.max(sj, axis=0, keepdims=True))
        alpha = jnp.exp2(m_old - m_new)
        l_new = alpha * l_sc[...]
        acc = alpha * acc_sc[...]
        for sj, v in zip(s, v_rows):
            pr = jnp.exp2(sj - m_new)
            l_new = l_new + jnp.sum(pr, axis=0, keepdims=True)
            acc = acc + lax.dot_general(v.astype(BF16), pr.astype(BF16), over_rows,
                                        preferred_element_type=F32)
        l_sc[...] = l_new
        acc_sc[...] = acc
        m_sc[...] = m_new

    last_kind = jnp.where(p == n_steps - 1, 1, 0)
    attend([r[...] for r in kc], [r[...] for r in vc],
           [mask_ref[0]] * (pp - 1) + [mask_ref[last_kind]])

    @pl.when(p == n_steps - 1)
    def _new_tokens():
        attend([kn_ref[...]], [vn_ref[...]], [maskn_ref[...]])
        lam = _lam_value(lamp_ref, lam_init)
        on = (acc_sc[...] * (1.0 / l_sc[...])).T
        for h in range(N_HEADS):
            r1 = h * grp
            o = on[r1:r1 + t_new, :] - lam * on[r1 + t_new:r1 + grp, :]
            ms = jnp.mean(o * o, axis=1, keepdims=True)
            y = o * lax.rsqrt(ms + RMS_EPS) * sg_ref[...] * (1.0 - lam_init)
            lanes = slice(h * V_DIM, (h + 1) * V_DIM)
            o_ref[:, lanes] = y * gz_ref[:, lanes]


def _sample_masks(rel_bias, page, t_new):
    cols = N_HEADS * 2 * t_new
    table = (rel_bias - rel_bias[N_BUCKETS - 1][None, :]) * LOG2E
    col = jnp.arange(cols, dtype=jnp.int32)[None, :]
    col_head, col_t = col // (2 * t_new), col % t_new

    def build(n_tok, first_distance):
        row = jnp.arange(n_tok * N_HEADS, dtype=jnp.int32)[:, None]
        tok, head = row // N_HEADS, row % N_HEADS
        dist = first_distance + col_t - tok
        per_head = _bias_by_distance(table, jnp.maximum(dist, 0))
        vals = per_head[0]
        for h in range(1, N_HEADS):
            vals = jnp.where(head == h, per_head[h], vals)
        return jnp.where((head == col_head) & (dist >= 0), vals, NEG_INF), head == col_head

    last, same_head = build(page, page)
    plain = jnp.where(same_head, 0.0, NEG_INF).astype(F32)
    new, _ = build(t_new, 0)
    return jnp.stack([plain, last]), new


def _attn_sample_call(qn, kn, v, gz, cache_k4, cache_v4, page_flat, layer, masks, mask_new, lamp, sg_row,
                      lam_init, db, t_new, n_pages, page):
    d = qn.shape[-1]
    cols = 2 * N_HEADS * t_new
    pp = SAMPLE_PAGES if n_pages % SAMPLE_PAGES == 0 else 1
    n_steps = n_pages // pp
    assert page + 1 >= _far_distance(), "every cached page but the last must sit in the last bias bucket"
    q5 = jnp.transpose(qn.reshape(db, t_new, N_HEADS, 2, HEAD_DIM), (0, 3, 4, 2, 1))
    eye = jnp.eye(2, dtype=BF16)
    qbdT = (q5[:, :, :, :, None, :] * eye[None, :, None, None, :, None]).reshape(db, V_DIM, cols)
    kn3 = kn[layer].reshape(db, t_new * N_HEADS, V_DIM)
    v3 = v[layer].reshape(db, t_new * N_HEADS, V_DIM)
    gz4 = gz.reshape(N_BRANCH, db, t_new, d)

    def page_spec(j):
        return pl.BlockSpec((None, None, page * N_HEADS, V_DIM),
                            lambda b, p, pt, j=j: (layer, pt[b * n_pages + p * pp + j], 0, 0))

    tok_spec = pl.BlockSpec((None, t_new * N_HEADS, V_DIM), lambda b, p, pt: (b, 0, 0))
    grid_spec = pltpu.PrefetchScalarGridSpec(
        num_scalar_prefetch=1,
        grid=(db, n_steps),
        in_specs=[pl.BlockSpec((None, V_DIM, cols), lambda b, p, pt: (b, 0, 0))]
        + [page_spec(j) for j in range(pp)] + [page_spec(j) for j in range(pp)]
        + [tok_spec, tok_spec,
           pl.BlockSpec(masks.shape, lambda b, p, pt: (0, 0, 0)),
           pl.BlockSpec(mask_new.shape, lambda b, p, pt: (0, 0)),
           pl.BlockSpec((None, None, t_new, d), lambda b, p, pt: (0, b, 0, 0)),
           _layer_spec(lamp, layer), _layer_spec(sg_row, layer)],
        out_specs=pl.BlockSpec((None, t_new, d), lambda b, p, pt: (b, 0, 0)),
        scratch_shapes=[pltpu.VMEM((1, cols), F32), pltpu.VMEM((1, cols), F32),
                        pltpu.VMEM((V_DIM, cols), F32)],
    )
    out = pl.pallas_call(
        functools.partial(_attn_sample_kernel, pp=pp, n_steps=n_steps, t_new=t_new, lam_init=lam_init),
        grid_spec=grid_spec,
        out_shape=jax.ShapeDtypeStruct((db, t_new, d), F32),
        compiler_params=_cparams("parallel", "arbitrary"),
        name="attn_sample",
    )(page_flat, qbdT, *([cache_k4] * pp), *([cache_v4] * pp), kn3, v3, masks, mask_new, gz4, lamp, sg_row)
    return out.reshape(db * t_new, d)


def _pool_kernel(u_ref, hist_ref, gz_ref, w_ref, scale_ref, o_ref, ext_sc, *, tm, pos0, carry):
    c = pl.program_id(1)
    nb = u_ref.shape[0]
    grp = u_ref.shape[2] // len(POOL_WINDOWS)

    @pl.when(c == 0)
    def _hist():
        ext_sc[:, 0:POOL_HIST, :] = hist_ref[...]

    ext_sc[:, POOL_HIST:POOL_HIST + tm, :] = u_ref[...]
    pos = pos0 + c * tm + lax.broadcasted_iota(jnp.int32, (nb, tm, grp), 1)
    for gi, win in enumerate(POOL_WINDOWS):
        lanes = slice(gi * grp, (gi + 1) * grp)
        tot = ext_sc[:, POOL_HIST:POOL_HIST + tm, lanes]
        for j in range(1, win):
            tot = tot + ext_sc[:, POOL_HIST - j:POOL_HIST - j + tm, lanes]
        cnt = jnp.minimum(pos + 1, win).astype(F32)
        pooled = tot / cnt - u_ref[:, :, lanes]
        mixed = jnp.dot(pooled.reshape(nb * tm, grp).astype(BF16), w_ref[gi],
                        preferred_element_type=F32).reshape(nb, tm, grp)
        o_ref[:, :, lanes] = (mixed * scale_ref[:, lanes] * gz_ref[:, :, lanes]).astype(o_ref.dtype)
    if carry:
        ext_sc[:, 0:POOL_HIST, :] = ext_sc[:, tm:tm + POOL_HIST, :]


def _pool_call(u4, hist, gz4, layer, pool_w_bf, pool_scale, pos0, nb):
    _, n, t, d = u4.shape
    tm = _tile(t, SEQ_TILE)
    nc = t // tm
    grp = d // len(POOL_WINDOWS)
    blk = lambda i, c: (i, c, 0)
    return pl.pallas_call(
        functools.partial(_pool_kernel, tm=tm, pos0=pos0, carry=nc > 1),
        grid=(n // nb, nc),
        in_specs=[pl.BlockSpec((None, nb, tm, d), lambda i, c: (0, i, c, 0)),
                  pl.BlockSpec((nb, POOL_HIST, d), lambda i, c: (i, 0, 0)),
                  pl.BlockSpec((None, nb, tm, d), lambda i, c: (1, i, c, 0)),
                  _layer_spec(pool_w_bf, layer), _layer_spec(pool_scale, layer)],
        out_specs=pl.BlockSpec((nb, tm, d), blk),
        out_shape=jax.ShapeDtypeStruct((n, t, d), gz4.dtype),
        scratch_shapes=[pltpu.VMEM((nb, tm + POOL_HIST, d), F32)],
        compiler_params=_cparams("parallel", "arbitrary"),
        name="pool_mix",
    )(u4, hist, gz4, pool_w_bf, pool_scale)


def _lru_kernel(u_ref, hist_ref, h0_ref, gz_ref, cw_ref, cb_ref, wa_ref, wx_ref, ba_ref, bx_ref,
                lam_ref, o_ref, hn_ref, ext_sc, h_sc, *, tm, carry):
    c = pl.program_id(1)
    nb, _, d = u_ref.shape

    @pl.when(c == 0)
    def _state():
        ext_sc[:, 0:CONV_HIST, :] = hist_ref[...]
        h_sc[...] = h0_ref[...]

    ext_sc[:, CONV_HIST:CONV_HIST + tm, :] = u_ref[...]
    xc = cb_ref[...] + ext_sc[:, CONV_HIST - (CONV_WIDTH - 1):CONV_HIST - (CONV_WIDTH - 1) + tm, :] * cw_ref[0:1, :]
    for j in range(1, CONV_WIDTH):
        lo = CONV_HIST - (CONV_WIDTH - 1) + j
        xc = xc + ext_sc[:, lo:lo + tm, :] * cw_ref[j:j + 1, :]
    xb = xc.reshape(nb * tm, d).astype(BF16)
    nt = d // GATE_TILE
    ra = jnp.concatenate([jnp.dot(xb[:, j * GATE_TILE:(j + 1) * GATE_TILE], wa_ref[j],
                                  preferred_element_type=F32) for j in range(nt)], axis=1)
    rx = jnp.concatenate([jnp.dot(xb[:, j * GATE_TILE:(j + 1) * GATE_TILE], wx_ref[j],
                                  preferred_element_type=F32) for j in range(nt)], axis=1)
    r = jax.nn.sigmoid(ra + ba_ref[...]).reshape(nb, tm, d)
    i = jax.nn.sigmoid(rx + bx_ref[...]).reshape(nb, tm, d)
    log_a = (-LRU_C) * r * jax.nn.softplus(-lam_ref[...])
    a = jnp.exp(log_a)
    bt = jnp.sqrt(-jnp.tanh(log_a) * (a * a + 1.0)) * (i * xc)
    row = lax.broadcasted_iota(jnp.int32, (nb, tm, d), 1) % SCAN_GROUP
    sft = 1
    while sft < SCAN_GROUP:
        keep = row >= sft
        a_prev = jnp.where(keep, pltpu.roll(a, sft, axis=1), 1.0)
        b_prev = jnp.where(keep, pltpu.roll(bt, sft, axis=1), 0.0)
        bt = a * b_prev + bt
        a = a * a_prev
        sft *= 2
    h_prev = h_sc[...]
    groups = []
    for g in range(tm // SCAN_GROUP):
        rows = slice(g * SCAN_GROUP, (g + 1) * SCAN_GROUP)
        hg = bt[:, rows, :] + a[:, rows, :] * h_prev
        groups.append(hg)
        h_prev = hg[:, SCAN_GROUP - 1:SCAN_GROUP, :]
    h = jnp.concatenate(groups, axis=1)
    h_last = h_prev
    h_sc[...] = h_last
    hn_ref[...] = h_last
    o_ref[...] = (h * gz_ref[...]).astype(o_ref.dtype)
    if carry:
        ext_sc[:, 0:CONV_HIST, :] = ext_sc[:, tm:tm + CONV_HIST, :]


def _lru_call(u4, hist, h0, gz4, layer, conv_w, conv_b, wa, wx, ba, bx, lam, nb):
    _, n, t, d = u4.shape
    tm = _tile(t, SEQ_TILE)
    nc = t // tm
    blk = lambda i, c: (i, c, 0)
    return pl.pallas_call(
        functools.partial(_lru_kernel, tm=tm, carry=nc > 1),
        grid=(n // nb, nc),
        in_specs=[pl.BlockSpec((None, nb, tm, d), lambda i, c: (1, i, c, 0)),
                  pl.BlockSpec((nb, CONV_HIST, d), lambda i, c: (i, 0, 0)),
                  pl.BlockSpec((nb, 1, d), lambda i, c: (i, 0, 0)),
                  pl.BlockSpec((None, nb, tm, d), lambda i, c: (2, i, c, 0)),
                  ] + [_layer_spec(a, layer) for a in (conv_w, conv_b, wa, wx, ba, bx, lam)],
        out_specs=[pl.BlockSpec((nb, tm, d), blk), pl.BlockSpec((nb, 1, d), lambda i, c: (i, 0, 0))],
        out_shape=[jax.ShapeDtypeStruct((n, t, d), gz4.dtype), jax.ShapeDtypeStruct((n, 1, d), F32)],
        scratch_shapes=[pltpu.VMEM((nb, tm + CONV_HIST, d), F32), pltpu.VMEM((nb, 1, d), F32)],
        compiler_params=_cparams("parallel", "arbitrary"),
        name="conv_rglru",
    )(u4, hist, h0, gz4, conv_w, conv_b, wa, wx, ba, bx, lam)


def _out_kernel(x_ref, ma_ref, mp_ref, ml_ref, w_ref, g_ref, y_ref, xn_ref):
    m = (ma_ref[...].astype(F32) + mp_ref[...].astype(F32) + ml_ref[...].astype(F32)).astype(BF16)
    y = x_ref[...] + jnp.dot(m, w_ref[...], preferred_element_type=F32)
    y_ref[...] = y
    xn_ref[...] = _rms_rows(y, g_ref[...]).astype(BF16)


def _out_call(x, ma, mp, ml, w_out_bf, layer, norm_g3):
    r, d = x.shape
    tm = _tile(r, ROW_TILE // 2)
    spec = pl.BlockSpec((tm, d), lambda i: (i, 0))
    return pl.pallas_call(
        _out_kernel, grid=(r // tm,),
        in_specs=[spec, spec, spec, spec,
                  pl.BlockSpec((None, d, d), lambda i: (layer, 0, 0)),
                  _layer_spec(norm_g3, (layer + 1) % norm_g3.shape[0])],
        out_specs=[spec, spec],
        out_shape=[jax.ShapeDtypeStruct((r, d), F32), jax.ShapeDtypeStruct((r, d), BF16)],
        compiler_params=_cparams("parallel"),
        name="out_proj",
    )(x, ma, mp, ml, w_out_bf, norm_g3)


def _block_diag(w, tile):
    n, c, _ = w.shape
    per = tile // c
    w4 = w.reshape(n // per, per, c, c)
    eye = jnp.eye(per, dtype=w.dtype)
    return (w4[:, :, :, None, :] * eye[None, :, None, :, None]).reshape(n // per, tile, tile)


def _mixers_and_out(x, xn, lw, layer, attn_fn, nseq, t, pos0, pool_hist, conv_hist, h0, nb, k_all, v_all, mid):
    d = x.shape[-1]
    qn, k_all, kb, v_all, vb, u, gz = _proj_calls(xn, lw["w_in"], layer, lw["pmat"], lw["gq"], lw["gk"],
                                                  k_all, v_all, mid)
    m_attn = attn_fn(qn, k_all, kb, v_all, vb, gz)
    u4 = u.reshape(2, nseq, t, d)
    gz4 = gz.reshape(N_BRANCH, nseq, t, d)
    m_pool = _pool_call(u4, pool_hist, gz4, layer, lw["pool_w"], lw["pool_scale"], pos0, nb)
    m_lru, h_new = _lru_call(u4, conv_hist, h0, gz4, layer, lw["conv_w"], lw["conv_b"],
                             lw["wa"], lw["wx"], lw["ba"], lw["bx"], lw["lam"], nb)
    y, xn_next = _out_call(x, m_attn, m_pool.reshape(nseq * t, d), m_lru.reshape(nseq * t, d),
                           lw["w_out"], layer, lw["norm_g"])
    return y, xn_next, k_all, v_all, u4, h_new.reshape(nseq, d)


def kernel(x_prompt, x_sample, cache_k, cache_v, page_table, state_pool, state_conv, state_h, rel_bias, norm_g, w_in, q_norm_g, k_norm_g, lam_q1, lam_k1, lam_q2, lam_k2, subln_g, pool_w, pool_scale, conv_w, conv_b, gate_a_w, gate_a_b, gate_x_w, gate_x_b, lru_lambda, w_out):
    b, s, d = x_prompt.shape
    db, t_new, _ = x_sample.shape
    depth = w_in.shape[0]
    n_phys, page = cache_k.shape[1], cache_k.shape[2]
    n_pages = page_table.shape[1]
    past = n_pages * page
    assert d == N_HEADS * V_DIM and d % GATE_TILE == 0 and GATE_TILE % (d // LRU_BLOCKS) == 0
    assert s >= POOL_BUF and cache_k.shape[3:] == (N_HEADS, V_DIM)

    w_in_bf = w_in.astype(BF16)
    w_out_bf = w_out.astype(BF16)
    pool_w_bf = pool_w.astype(BF16)
    pmat = jnp.kron(jnp.eye(d // HEAD_DIM, dtype=F32), jnp.ones((HEAD_DIM, HEAD_DIM), F32)).astype(BF16)
    reps = d // HEAD_DIM
    cache_k4 = cache_k.reshape(depth, n_phys, page * N_HEADS, V_DIM)
    cache_v4 = cache_v.reshape(depth, n_phys, page * N_HEADS, V_DIM)
    page_flat = page_table.reshape(-1).astype(jnp.int32)

    t_attn = _tile(s, ATTN_TILE)
    bias_tiles = _prompt_bias_tiles(rel_bias, t_attn)
    masks_s, mask_new = _sample_masks(rel_bias, page, t_new)

    pool0 = jnp.zeros((b, POOL_HIST, d), F32)
    conv0 = jnp.zeros((b, CONV_HIST, d), F32)
    h00 = jnp.zeros((b, 1, d), F32)
    nb_s = 16 if db % 16 == 0 else 1

    xp = x_prompt.reshape(b * s, d)
    xs = x_sample.reshape(db * t_new, d)
    n_gate = LRU_BLOCKS * (d // LRU_BLOCKS) // GATE_TILE
    lw = {
        "w_in": w_in_bf, "w_out": w_out_bf, "pmat": pmat, "norm_g": norm_g[:, None, :],
        "gq": jnp.tile(q_norm_g * (HEAD_DIM ** -0.5 * LOG2E), (1, reps))[:, None, :],
        "gk": jnp.tile(k_norm_g, (1, reps))[:, None, :],
        "pool_w": pool_w_bf, "pool_scale": pool_scale[:, None, :],
        "conv_w": conv_w, "conv_b": conv_b[:, None, :],
        "wa": _block_diag(gate_a_w.reshape((-1,) + gate_a_w.shape[2:]), GATE_TILE).astype(BF16)
        .reshape(depth, n_gate, GATE_TILE, GATE_TILE),
        "wx": _block_diag(gate_x_w.reshape((-1,) + gate_x_w.shape[2:]), GATE_TILE).astype(BF16)
        .reshape(depth, n_gate, GATE_TILE, GATE_TILE),
        "ba": gate_a_b.reshape(depth, 1, d), "bx": gate_x_b.reshape(depth, 1, d),
        "lam": lru_lambda[:, None, :],
    }
    lamp = jnp.stack([lam_q1, lam_k1, lam_q2, lam_k2], axis=1)
    sg_col = subln_g[:, :, None]
    sg_row = subln_g[:, None, :]
    pool_hist = jnp.concatenate([jnp.zeros((depth, db, POOL_HIST - POOL_BUF, d), F32), state_pool], axis=2)
    conv_hist = jnp.concatenate([jnp.zeros((depth, db, CONV_HIST - (CONV_WIDTH - 1), d), F32), state_conv],
                                axis=2)

    xnp = _norm_call(xp, lw["norm_g"])
    xns = _norm_call(xs, lw["norm_g"])

    kp = vp = ksm = vsm = None
    outs = [[] for _ in range(6)]
    for l in range(depth):
        lam_init = 0.8 - 0.6 * math.exp(-0.3 * l)

        attn_p = lambda qn, kn, kb, v, vb, gz: _attn_prompt_call(
            qn, kb, vb, gz, bias_tiles, l, lamp, sg_col, lam_init, b, s)
        xp, xnp, kp, vp, up4, hp = _mixers_and_out(
            xp, xnp, lw, l, attn_p, b, s, 0, pool0, conv0, h00, 1, kp, vp, BF16)

        attn_s = lambda qn, kn, kb, v, vb, gz: _attn_sample_call(
            qn, kn, v, gz, cache_k4, cache_v4, page_flat, l, masks_s, mask_new, lamp, sg_row,
            lam_init, db, t_new, n_pages, page)
        xs, xns, ksm, vsm, us4, hs = _mixers_and_out(
            xs, xns, lw, l, attn_s, db, t_new, past, pool_hist[l], conv_hist[l], state_h[l][:, None], nb_s,
            ksm, vsm, F32)

        new = (up4[0, :, -POOL_BUF:],
               jnp.concatenate([state_pool[l], us4[0]], axis=1)[:, -POOL_BUF:],
               up4[1, :, -(CONV_WIDTH - 1):],
               jnp.concatenate([state_conv[l], us4[1]], axis=1)[:, -(CONV_WIDTH - 1):],
               hp, hs)
        for acc, val in zip(outs, new):
            acc.append(val)
    return ((xp.reshape(b, s, d), xs.reshape(db, t_new, d),
             kp.reshape(depth, b, s, N_HEADS, V_DIM), vp.reshape(depth, b, s, N_HEADS, V_DIM),
             ksm.reshape(depth, db, t_new, N_HEADS, V_DIM), vsm.reshape(depth, db, t_new, N_HEADS, V_DIM))
            + tuple(jnp.stack(o) for o in outs))
```

```python
import functools
import math

import numpy as np
import jax
import jax.numpy as jnp
from jax import lax
from jax.experimental import pallas as pl
from jax.experimental.pallas import tpu as pltpu

F32 = jnp.float32
BF16 = jnp.bfloat16

RMS_EPS = 1e-6
NEG_INF = -1e30
LOG2E = 1.4426950408889634

N_HEADS = 8
HEAD_DIM = 64
V_DIM = 2 * HEAD_DIM
N_BUCKETS = 32
MAX_DISTANCE = 128
POOL_WINDOWS = (2, 4, 8, 16)
POOL_BUF = max(POOL_WINDOWS) - 1
POOL_HIST = 16
CONV_WIDTH = 4
CONV_HIST = 8
LRU_BLOCKS = 16
LRU_C = 8.0
N_BRANCH = 3
GATE_TILE = 256

ROW_TILE = 1024
ATTN_TILE = 512
FAR_UNROLL = 4
SEQ_TILE = 256
SCAN_GROUP = 8
SAMPLE_PAGES = 8
VMEM_LIMIT = 56 * 1024 * 1024


def _cparams(*sem):
    return pltpu.CompilerParams(dimension_semantics=sem, vmem_limit_bytes=VMEM_LIMIT)


def _tile(n, pref):
    return pref if n % pref == 0 else n


def _layer_spec(a, layer):
    zeros = (0,) * (a.ndim - 1)
    return pl.BlockSpec((None,) + a.shape[1:], lambda *_: (layer,) + zeros)


def _rms_rows(x, g):
    ms = jnp.mean(x * x, axis=-1, keepdims=True)
    return x * lax.rsqrt(ms + RMS_EPS) * g


def _norm_kernel(x_ref, g_ref, o_ref):
    o_ref[...] = _rms_rows(x_ref[...], g_ref[...]).astype(BF16)


def _norm_call(x, g):
    r, d = x.shape
    tm = _tile(r, ROW_TILE)
    return pl.pallas_call(
        _norm_kernel,
        grid=(r // tm,),
        in_specs=[pl.BlockSpec((tm, d), lambda i: (i, 0)), _layer_spec(g, 0)],
        out_specs=pl.BlockSpec((tm, d), lambda i: (i, 0)),
        out_shape=jax.ShapeDtypeStruct((r, d), BF16),
        compiler_params=_cparams("parallel"),
        name="prenorm",
    )(x, g)


def _head_norm(acc, pmat_ref, g_ref):
    ss = jnp.dot((acc * acc).astype(BF16), pmat_ref[...], preferred_element_type=F32)
    return acc * lax.rsqrt(ss * (1.0 / HEAD_DIM) + RMS_EPS) * g_ref[...]


def _proj_q_kernel(xn_ref, w_ref, pmat_ref, g_ref, q_ref):
    acc = jnp.dot(xn_ref[...], w_ref[...], preferred_element_type=F32)
    q_ref[...] = _head_norm(acc, pmat_ref, g_ref).astype(BF16)


def _proj_k_kernel(xn_ref, w_ref, pmat_ref, g_ref, *rest):
    k_ref, kb_ref = rest[-2:]
    acc = jnp.dot(xn_ref[...], w_ref[...], preferred_element_type=F32)
    kn = _head_norm(acc, pmat_ref, g_ref)
    k_ref[...] = kn
    kb_ref[...] = kn.astype(BF16)


def _proj_v_kernel(xn_ref, w_ref, *rest):
    v_ref, vb_ref = rest[-2:]
    acc = jnp.dot(xn_ref[...], w_ref[...], preferred_element_type=F32)
    v_ref[...] = acc
    vb_ref[...] = acc.astype(BF16)


def _proj_u_kernel(xn_ref, w_ref, u_ref):
    u_ref[...] = jnp.dot(xn_ref[...], w_ref[...], preferred_element_type=F32)


def _proj_gz_kernel(xn_ref, wz_ref, wg_ref, o_ref):
    xn = xn_ref[...]
    z = jnp.dot(xn, wz_ref[...], preferred_element_type=F32)
    g = jnp.dot(xn, wg_ref[...], preferred_element_type=F32)
    o_ref[...] = (jax.nn.sigmoid(g) * (z * jax.nn.sigmoid(z))).astype(o_ref.dtype)


def _proj_calls(xn, w_bf, layer, pmat, gq, gk, k_all, v_all, mid):
    r, d = xn.shape
    depth = w_bf.shape[0]
    tm = _tile(r, ROW_TILE)
    nr = r // tm
    x_spec = pl.BlockSpec((tm, d), lambda i: (i, 0))
    full = lambda a: pl.BlockSpec(a.shape, lambda i: (0,) * a.ndim)
    o_spec = pl.BlockSpec((tm, d), lambda i: (i, 0))
    all_spec = pl.BlockSpec((None, tm, d), lambda i: (layer, i, 0))
    all_shape = jax.ShapeDtypeStruct((depth, r, d), F32)
    any_spec = pl.BlockSpec(memory_space=pl.ANY)

    def w_spec(j):
        return pl.BlockSpec((None, d, d), lambda i: (layer, 0, j))

    qn = pl.pallas_call(
        _proj_q_kernel, grid=(nr,),
        in_specs=[x_spec, w_spec(0), full(pmat), _layer_spec(gq, layer)],
        out_specs=o_spec, out_shape=jax.ShapeDtypeStruct((r, d), BF16),
        compiler_params=_cparams("parallel"), name="proj_q")(xn, w_bf, pmat, gq)
    k_prev = () if k_all is None else (k_all,)
    k_all, kb = pl.pallas_call(
        _proj_k_kernel, grid=(nr,),
        in_specs=[x_spec, w_spec(1), full(pmat), _layer_spec(gk, layer)] + [any_spec] * len(k_prev),
        out_specs=[all_spec, o_spec],
        out_shape=[all_shape, jax.ShapeDtypeStruct((r, d), BF16)],
        input_output_aliases={4: 0} if k_prev else {},
        compiler_params=_cparams("parallel"), name="proj_k")(xn, w_bf, pmat, gk, *k_prev)
    v_prev = () if v_all is None else (v_all,)
    v_all, vb = pl.pallas_call(
        _proj_v_kernel, grid=(nr,),
        in_specs=[x_spec, w_spec(2)] + [any_spec] * len(v_prev),
        out_specs=[all_spec, o_spec],
        out_shape=[all_shape, jax.ShapeDtypeStruct((r, d), BF16)],
        input_output_aliases={2: 0} if v_prev else {},
        compiler_params=_cparams("parallel"), name="proj_v")(xn, w_bf, *v_prev)
    u = pl.pallas_call(
        _proj_u_kernel, grid=(2, nr),
        in_specs=[pl.BlockSpec((tm, d), lambda j, i: (i, 0)),
                  pl.BlockSpec((None, d, d), lambda j, i: (layer, 0, 3 + j))],
        out_specs=pl.BlockSpec((None, tm, d), lambda j, i: (j, i, 0)),
        out_shape=jax.ShapeDtypeStruct((2, r, d), F32),
        compiler_params=_cparams("parallel", "parallel"), name="proj_u")(xn, w_bf)
    gz = pl.pallas_call(
        _proj_gz_kernel, grid=(N_BRANCH, nr),
        in_specs=[pl.BlockSpec((tm, d), lambda j, i: (i, 0)),
                  pl.BlockSpec((None, d, d), lambda j, i: (layer, 0, 5 + j)),
                  pl.BlockSpec((None, d, d), lambda j, i: (layer, 0, 8 + j))],
        out_specs=pl.BlockSpec((None, tm, d), lambda j, i: (j, i, 0)),
        out_shape=jax.ShapeDtypeStruct((N_BRANCH, r, d), mid),
        compiler_params=_cparams("parallel", "parallel"), name="proj_gz")(xn, w_bf, w_bf)
    return qn, k_all, kb, v_all, vb, u, gz


def _lam_value(lamp_ref, lam_init):
    lp = lamp_ref[...]
    s1 = jnp.sum(lp[0:1] * lp[1:2], axis=1, keepdims=True)
    s2 = jnp.sum(lp[2:3] * lp[3:4], axis=1, keepdims=True)
    return jnp.exp(s1) - jnp.exp(s2) + lam_init


def _bias_by_distance(table, n):
    max_exact = N_BUCKETS // 2
    nf = jnp.maximum(n, 1).astype(F32)
    large = max_exact + (jnp.log(nf / max_exact) / math.log(MAX_DISTANCE / max_exact)
                         * (N_BUCKETS - max_exact)).astype(jnp.int32)
    large = jnp.minimum(large, N_BUCKETS - 1)
    bucket = jnp.where(n < max_exact, n, large)[None]
    col = lambda b: table[b].reshape((-1,) + (1,) * n.ndim)
    out = jnp.broadcast_to(col(0), (table.shape[1],) + n.shape)
    for b in range(1, N_BUCKETS):
        out = jnp.where(bucket == b, col(b), out)
    return out


def _far_distance():
    max_exact = N_BUCKETS // 2
    n = np.arange(1, 4 * MAX_DISTANCE, dtype=np.int64)
    large = max_exact + (np.log(n.astype(np.float32) / np.float32(max_exact))
                         / np.float32(math.log(MAX_DISTANCE / max_exact))
                         * (N_BUCKETS - max_exact)).astype(np.int64)
    not_last = np.nonzero((n < max_exact) | (large < N_BUCKETS - 1))[0]
    return int(n[not_last[-1]] + 1) + 2


def _attn_prompt_kernel(q_ref, k_ref, v_ref, bias_ref, gz_ref, lamp_ref, sg_ref, o_ref,
                        vT_sc, qp_sc, s_sc, m_sc, l_sc, acc_sc, *, t, n_blk, lam_init):
    qi = pl.program_id(2)

    @pl.when(qi == 0)
    def _transpose_values():
        def body(c, carry):
            r0 = pl.multiple_of(c * t, t)
            vT_sc[:, pl.ds(r0, t)] = v_ref[pl.ds(r0, t), :].astype(F32).T.astype(BF16)
            return carry
        lax.fori_loop(0, n_blk, body, 0)

    qT = q_ref[...].astype(F32).T
    row = lax.broadcasted_iota(jnp.int32, qT.shape, 0)
    qp_sc[:, :t] = jnp.where(row < HEAD_DIM, qT, 0.0).astype(BF16)
    qp_sc[:, t:] = jnp.where(row >= HEAD_DIM, qT, 0.0).astype(BF16)
    m_sc[...] = jnp.full(m_sc.shape, NEG_INF, F32)
    l_sc[...] = jnp.zeros(l_sc.shape, F32)
    acc_sc[...] = jnp.zeros(acc_sc.shape, F32)

    def scores(ki, cj, bias_idx):
        r0 = pl.multiple_of(ki * t, t)
        s = jnp.dot(k_ref[pl.ds(r0, t), :], qp_sc[:, cj * t:(cj + 1) * t], preferred_element_type=F32)
        if bias_idx is not None:
            s = s + bias_ref[bias_idx]
        s_sc[cj] = s

    def update(ki, cj):
        r0 = pl.multiple_of(ki * t, t)
        cols = slice(cj * t, (cj + 1) * t)
        s = s_sc[cj]
        m_old = m_sc[:, cols]
        m_new = jnp.maximum(m_old, jnp.max(s, axis=0, keepdims=True))
        alpha = jnp.exp2(m_old - m_new)
        p = jnp.exp2(s - m_new)
        l_sc[:, cols] = alpha * l_sc[:, cols] + jnp.sum(p, axis=0, keepdims=True)
        acc_sc[:, cols] = alpha * acc_sc[:, cols] + jnp.dot(
            vT_sc[:, pl.ds(r0, t)], p.astype(BF16), preferred_element_type=F32)
        m_sc[:, cols] = m_new

    scores(0, 0, None)

    def far_block(ki):
        scores(ki, 1, None)
        update(ki, 0)
        scores(ki + 1, 0, None)
        update(ki, 1)

    def far_group(j, carry):
        for i in range(FAR_UNROLL):
            far_block(FAR_UNROLL * j + i)
        return carry

    n_far = jnp.maximum(qi - 1, 0)
    lax.fori_loop(0, n_far // FAR_UNROLL, far_group, 0)

    def far_rest(ki, carry):
        far_block(ki)
        return carry

    lax.fori_loop(n_far - n_far % FAR_UNROLL, n_far, far_rest, 0)

    @pl.when(qi >= 1)
    def _below_diagonal():
        s_sc[0] = s_sc[0] + bias_ref[1]
        scores(qi - 1, 1, 1)
        update(qi - 1, 0)
        scores(qi, 0, 0)
        update(qi - 1, 1)

    @pl.when(qi == 0)
    def _first_block():
        s_sc[0] = s_sc[0] + bias_ref[0]

    scores(qi, 1, 0)
    update(qi, 0)
    update(qi, 1)

    lam = _lam_value(lamp_ref, lam_init)
    on = acc_sc[...] * (1.0 / l_sc[...])
    o = on[:, :t] - lam * on[:, t:]
    ms = jnp.mean(o * o, axis=0, keepdims=True)
    y = o * lax.rsqrt(ms + RMS_EPS) * sg_ref[...] * (1.0 - lam_init)
    o_ref[...] = (y.T * gz_ref[...]).astype(o_ref.dtype)


def _prompt_bias_tiles(rel_bias, t):
    table = (rel_bias - rel_bias[N_BUCKETS - 1][None, :]) * LOG2E
    kk = jnp.arange(t, dtype=jnp.int32)[:, None]
    qq = jnp.arange(t, dtype=jnp.int32)[None, :]
    dist = jnp.stack([jnp.maximum(qq - kk, 0), t + qq - kk])
    tiles = _bias_by_distance(table, dist)
    return jnp.where((jnp.stack([qq - kk, t + qq - kk]) >= 0)[None], tiles, NEG_INF)


def _attn_prompt_call(qn, kb, vb, gz, bias_tiles, layer, lamp, sg_col, lam_init, b, s):
    d = qn.shape[-1]
    t = _tile(s, ATTN_TILE)
    assert t + 1 >= _far_distance(), "key blocks two or more below the diagonal must sit in the last bias bucket"
    q3, k3, v3 = (a.reshape(b, s, d) for a in (qn, kb, vb))
    gz4 = gz.reshape(N_BRANCH, b, s, d)
    out = pl.pallas_call(
        functools.partial(_attn_prompt_kernel, t=t, n_blk=s // t, lam_init=lam_init),
        grid=(b, N_HEADS, s // t),
        in_specs=[
            pl.BlockSpec((None, t, V_DIM), lambda bi, h, qi: (bi, qi, h)),
            pl.BlockSpec((None, s, V_DIM), lambda bi, h, qi: (bi, 0, h)),
            pl.BlockSpec((None, s, V_DIM), lambda bi, h, qi: (bi, 0, h)),
            pl.BlockSpec((None, 2, t, t), lambda bi, h, qi: (h, 0, 0, 0)),
            pl.BlockSpec((None, None, t, V_DIM), lambda bi, h, qi: (0, bi, qi, h)),
            _layer_spec(lamp, layer),
            _layer_spec(sg_col, layer),
        ],
        out_specs=pl.BlockSpec((None, t, V_DIM), lambda bi, h, qi: (bi, qi, h)),
        out_shape=jax.ShapeDtypeStruct((b, s, d), gz.dtype),
        scratch_shapes=[pltpu.VMEM((V_DIM, s), BF16), pltpu.VMEM((V_DIM, 2 * t), BF16),
                        pltpu.VMEM((2, t, t), F32),
                        pltpu.VMEM((1, 2 * t), F32), pltpu.VMEM((1, 2 * t), F32),
                        pltpu.VMEM((V_DIM, 2 * t), F32)],
        compiler_params=_cparams("parallel", "parallel", "arbitrary"),
        name="attn_prompt",
    )(q3, k3, v3, bias_tiles, gz4, lamp, sg_col)
    return out.reshape(b * s, d)


def _attn_sample_kernel(pt_ref, qbdT_ref, *refs, pp, n_steps, t_new, lam_init):
    kc = refs[:pp]
    vc = refs[pp:2 * pp]
    (kn_ref, vn_ref, mask_ref, maskn_ref, gz_ref, lamp_ref, sg_ref, o_ref, m_sc, l_sc, acc_sc) = refs[2 * pp:]
    p = pl.program_id(1)
    grp = 2 * t_new
    over_rows = (((0,), (0,)), ((), ()))

    @pl.when(p == 0)
    def _init():
        m_sc[...] = jnp.full(m_sc.shape, NEG_INF, F32)
        l_sc[...] = jnp.zeros(l_sc.shape, F32)
        acc_sc[...] = jnp.zeros(acc_sc.shape, F32)

    def attend(k_rows, v_rows, masks):
        s = [jnp.dot(k.astype(BF16), qbdT_ref[...], preferred_element_type=F32) + mk
             for k, mk in zip(k_rows, masks)]
        m_old = m_sc[...]
        m_new = m_old
        for sj in s:
            m_new = jnp.maximum(m_new, jnp.max(sj, axis=0, keepdims=True))
        alpha = jnp.exp2(m_old - m_new)
        l_new = alpha * l_sc[...]
        acc = alpha * acc_sc[...]
        for sj, v in zip(s, v_rows):
            pr = jnp.exp2(sj - m_new)
            l_new = l_new + jnp.sum(pr, axis=0, keepdims=True)
            acc = acc + lax.dot_general(v.astype(BF16), pr.astype(BF16), over_rows,
                                        preferred_element_type=F32)
        l_sc[...] = l_new
        acc_sc[...] = acc
        m_sc[...] = m_new

    last_kind = jnp.where(p == n_steps - 1, 1, 0)
    attend([r[...] for r in kc], [r[...] for r in vc],
           [mask_ref[0]] * (pp - 1) + [mask_ref[last_kind]])

    @pl.when(p == n_steps - 1)
    def _new_tokens():
        attend([kn_ref[...]], [vn_ref[...]], [maskn_ref[...]])
        lam = _lam_value(lamp_ref, lam_init)
        on = (acc_sc[...] * (1.0 / l_sc[...])).T
        for h in range(N_HEADS):
            r1 = h * grp
            o = on[r1:r1 + t_new, :] - lam * on[r1 + t_new:r1 + grp, :]
            ms = jnp.mean(o * o, axis=1, keepdims=True)
            y = o * lax.rsqrt(ms + RMS_EPS) * sg_ref[...] * (1.0 - lam_init)
            lanes = slice(h * V_DIM, (h + 1) * V_DIM)
            o_ref[:, lanes] = y * gz_ref[:, lanes]


def _sample_masks(rel_bias, page, t_new):
    cols = N_HEADS * 2 * t_new
    table = (rel_bias - rel_bias[N_BUCKETS - 1][None, :]) * LOG2E
    col = jnp.arange(cols, dtype=jnp.int32)[None, :]
    col_head, col_t = col // (2 * t_new), col % t_new

    def build(n_tok, first_distance):
        row = jnp.arange(n_tok * N_HEADS, dtype=jnp.int32)[:, None]
        tok, head = row // N_HEADS, row % N_HEADS
        dist = first_distance + col_t - tok
        per_head = _bias_by_distance(table, jnp.maximum(dist, 0))
        vals = per_head[0]
        for h in range(1, N_HEADS):
            vals = jnp.where(head == h, per_head[h], vals)
        return jnp.where((head == col_head) & (dist >= 0), vals, NEG_INF), head == col_head

    last, same_head = build(page, page)
    plain = jnp.where(same_head, 0.0, NEG_INF).astype(F32)
    new, _ = build(t_new, 0)
    return jnp.stack([plain, last]), new


def _attn_sample_call(qn, kn, v, gz, cache_k4, cache_v4, page_flat, layer, masks, mask_new, lamp, sg_row,
                      lam_init, db, t_new, n_pages, page):
    d = qn.shape[-1]
    cols = 2 * N_HEADS * t_new
    pp = SAMPLE_PAGES if n_pages % SAMPLE_PAGES == 0 else 1
    n_steps = n_pages // pp
    assert page + 1 >= _far_distance(), "every cached page but the last must sit in the last bias bucket"
    q5 = jnp.transpose(qn.reshape(db, t_new, N_HEADS, 2, HEAD_DIM), (0, 3, 4, 2, 1))
    eye = jnp.eye(2, dtype=BF16)
    qbdT = (q5[:, :, :, :, None, :] * eye[None, :, None, None, :, None]).reshape(db, V_DIM, cols)
    kn3 = kn[layer].reshape(db, t_new * N_HEADS, V_DIM)
    v3 = v[layer].reshape(db, t_new * N_HEADS, V_DIM)
    gz4 = gz.reshape(N_BRANCH, db, t_new, d)

    def page_spec(j):
        return pl.BlockSpec((None, None, page * N_HEADS, V_DIM),
                            lambda b, p, pt, j=j: (layer, pt[b * n_pages + p * pp + j], 0, 0))

    tok_spec = pl.BlockSpec((None, t_new * N_HEADS, V_DIM), lambda b, p, pt: (b, 0, 0))
    grid_spec = pltpu.PrefetchScalarGridSpec(
        num_scalar_prefetch=1,
        grid=(db, n_steps),
        in_specs=[pl.BlockSpec((None, V_DIM, cols), lambda b, p, pt: (b, 0, 0))]
        + [page_spec(j) for j in range(pp)] + [page_spec(j) for j in range(pp)]
        + [tok_spec, tok_spec,
           pl.BlockSpec(masks.shape, lambda b, p, pt: (0, 0, 0)),
           pl.BlockSpec(mask_new.shape, lambda b, p, pt: (0, 0)),
           pl.BlockSpec((None, None, t_new, d), lambda b, p, pt: (0, b, 0, 0)),
           _layer_spec(lamp, layer), _layer_spec(sg_row, layer)],
        out_specs=pl.BlockSpec((None, t_new, d), lambda b, p, pt: (b, 0, 0)),
        scratch_shapes=[pltpu.VMEM((1, cols), F32), pltpu.VMEM((1, cols), F32),
                        pltpu.VMEM((V_DIM, cols), F32)],
    )
    out = pl.pallas_call(
        functools.partial(_attn_sample_kernel, pp=pp, n_steps=n_steps, t_new=t_new, lam_init=lam_init),
        grid_spec=grid_spec,
        out_shape=jax.ShapeDtypeStruct((db, t_new, d), F32),
        compiler_params=_cparams("parallel", "arbitrary"),
        name="attn_sample",
    )(page_flat, qbdT, *([cache_k4] * pp), *([cache_v4] * pp), kn3, v3, masks, mask_new, gz4, lamp, sg_row)
    return out.reshape(db * t_new, d)


def _pool_kernel(u_ref, hist_ref, gz_ref, w_ref, scale_ref, o_ref, ext_sc, *, tm, pos0, carry):
    c = pl.program_id(1)
    nb = u_ref.shape[0]
    grp = u_ref.shape[2] // len(POOL_WINDOWS)

    @pl.when(c == 0)
    def _hist():
        ext_sc[:, 0:POOL_HIST, :] = hist_ref[...]

    ext_sc[:, POOL_HIST:POOL_HIST + tm, :] = u_ref[...]
    pos = pos0 + c * tm + lax.broadcasted_iota(jnp.int32, (nb, tm, grp), 1)
    for gi, win in enumerate(POOL_WINDOWS):
        lanes = slice(gi * grp, (gi + 1) * grp)
        tot = ext_sc[:, POOL_HIST:POOL_HIST + tm, lanes]
        for j in range(1, win):
            tot = tot + ext_sc[:, POOL_HIST - j:POOL_HIST - j + tm, lanes]
        cnt = jnp.minimum(pos + 1, win).astype(F32)
        pooled = tot / cnt - u_ref[:, :, lanes]
        mixed = jnp.dot(pooled.reshape(nb * tm, grp).astype(BF16), w_ref[gi],
                        preferred_element_type=F32).reshape(nb, tm, grp)
        o_ref[:, :, lanes] = (mixed * scale_ref[:, lanes] * gz_ref[:, :, lanes]).astype(o_ref.dtype)
    if carry:
        ext_sc[:, 0:POOL_HIST, :] = ext_sc[:, tm:tm + POOL_HIST, :]


def _pool_call(u4, hist, gz4, layer, pool_w_bf, pool_scale, pos0, nb):
    _, n, t, d = u4.shape
    tm = _tile(t, SEQ_TILE)
    nc = t // tm
    grp = d // len(POOL_WINDOWS)
    blk = lambda i, c: (i, c, 0)
    return pl.pallas_call(
        functools.partial(_pool_kernel, tm=tm, pos0=pos0, carry=nc > 1),
        grid=(n // nb, nc),
        in_specs=[pl.BlockSpec((None, nb, tm, d), lambda i, c: (0, i, c, 0)),
                  pl.BlockSpec((nb, POOL_HIST, d), lambda i, c: (i, 0, 0)),
                  pl.BlockSpec((None, nb, tm, d), lambda i, c: (1, i, c, 0)),
                  _layer_spec(pool_w_bf, layer), _layer_spec(pool_scale, layer)],
        out_specs=pl.BlockSpec((nb, tm, d), blk),
        out_shape=jax.ShapeDtypeStruct((n, t, d), gz4.dtype),
        scratch_shapes=[pltpu.VMEM((nb, tm + POOL_HIST, d), F32)],
        compiler_params=_cparams("parallel", "arbitrary"),
        name="pool_mix",
    )(u4, hist, gz4, pool_w_bf, pool_scale)


def _lru_kernel(u_ref, hist_ref, h0_ref, gz_ref, cw_ref, cb_ref, wa_ref, wx_ref, ba_ref, bx_ref,
                lam_ref, o_ref, hn_ref, ext_sc, h_sc, *, tm, carry):
    c = pl.program_id(1)
    nb, _, d = u_ref.shape

    @pl.when(c == 0)
    def _state():
        ext_sc[:, 0:CONV_HIST, :] = hist_ref[...]
        h_sc[...] = h0_ref[...]

    ext_sc[:, CONV_HIST:CONV_HIST + tm, :] = u_ref[...]
    xc = cb_ref[...] + ext_sc[:, CONV_HIST - (CONV_WIDTH - 1):CONV_HIST - (CONV_WIDTH - 1) + tm, :] * cw_ref[0:1, :]
    for j in range(1, CONV_WIDTH):
        lo = CONV_HIST - (CONV_WIDTH - 1) + j
        xc = xc + ext_sc[:, lo:lo + tm, :] * cw_ref[j:j + 1, :]
    xb = xc.reshape(nb * tm, d).astype(BF16)
    nt = d // GATE_TILE
    ra = jnp.concatenate([jnp.dot(xb[:, j * GATE_TILE:(j + 1) * GATE_TILE], wa_ref[j],
                                  preferred_element_type=F32) for j in range(nt)], axis=1)
    rx = jnp.concatenate([jnp.dot(xb[:, j * GATE_TILE:(j + 1) * GATE_TILE], wx_ref[j],
                                  preferred_element_type=F32) for j in range(nt)], axis=1)
    r = jax.nn.sigmoid(ra + ba_ref[...]).reshape(nb, tm, d)
    i = jax.nn.sigmoid(rx + bx_ref[...]).reshape(nb, tm, d)
    log_a = (-LRU_C) * r * jax.nn.softplus(-lam_ref[...])
    a = jnp.exp(log_a)
    bt = jnp.sqrt(-jnp.tanh(log_a) * (a * a + 1.0)) * (i * xc)
    row = lax.broadcasted_iota(jnp.int32, (nb, tm, d), 1) % SCAN_GROUP
    sft = 1
    while sft < SCAN_GROUP:
        keep = row >= sft
        a_prev = jnp.where(keep, pltpu.roll(a, sft, axis=1), 1.0)
        b_prev = jnp.where(keep, pltpu.roll(bt, sft, axis=1), 0.0)
        bt = a * b_prev + bt
        a = a * a_prev
        sft *= 2
    h_prev = h_sc[...]
    groups = []
    for g in range(tm // SCAN_GROUP):
        rows = slice(g * SCAN_GROUP, (g + 1) * SCAN_GROUP)
        hg = bt[:, rows, :] + a[:, rows, :] * h_prev
        groups.append(hg)
        h_prev = hg[:, SCAN_GROUP - 1:SCAN_GROUP, :]
    h = jnp.concatenate(groups, axis=1)
    h_last = h_prev
    h_sc[...] = h_last
    hn_ref[...] = h_last
    o_ref[...] = (h * gz_ref[...]).astype(o_ref.dtype)
    if carry:
        ext_sc[:, 0:CONV_HIST, :] = ext_sc[:, tm:tm + CONV_HIST, :]


def _lru_call(u4, hist, h0, gz4, layer, conv_w, conv_b, wa, wx, ba, bx, lam, nb):
    _, n, t, d = u4.shape
    tm = _tile(t, SEQ_TILE)
    nc = t // tm
    blk = lambda i, c: (i, c, 0)
    return pl.pallas_call(
        functools.partial(_lru_kernel, tm=tm, carry=nc > 1),
        grid=(n // nb, nc),
        in_specs=[pl.BlockSpec((None, nb, tm, d), lambda i, c: (1, i, c, 0)),
                  pl.BlockSpec((nb, CONV_HIST, d), lambda i, c: (i, 0, 0)),
                  pl.BlockSpec((nb, 1, d), lambda i, c: (i, 0, 0)),
                  pl.BlockSpec((None, nb, tm, d), lambda i, c: (2, i, c, 0)),
                  ] + [_layer_spec(a, layer) for a in (conv_w, conv_b, wa, wx, ba, bx, lam)],
        out_specs=[pl.BlockSpec((nb, tm, d), blk), pl.BlockSpec((nb, 1, d), lambda i, c: (i, 0, 0))],
        out_shape=[jax.ShapeDtypeStruct((n, t, d), gz4.dtype), jax.ShapeDtypeStruct((n, 1, d), F32)],
        scratch_shapes=[pltpu.VMEM((nb, tm + CONV_HIST, d), F32), pltpu.VMEM((nb, 1, d), F32)],
        compiler_params=_cparams("parallel", "arbitrary"),
        name="conv_rglru",
    )(u4, hist, h0, gz4, conv_w, conv_b, wa, wx, ba, bx, lam)


def _out_kernel(x_ref, ma_ref, mp_ref, ml_ref, w_ref, g_ref, y_ref, xn_ref):
    m = (ma_ref[...].astype(F32) + mp_ref[...].astype(F32) + ml_ref[...].astype(F32)).astype(BF16)
    y = x_ref[...] + jnp.dot(m, w_ref[...], preferred_element_type=F32)
    y_ref[...] = y
    xn_ref[...] = _rms_rows(y, g_ref[...]).astype(BF16)


def _out_call(x, ma, mp, ml, w_out_bf, layer, norm_g3):
    r, d = x.shape
    tm = _tile(r, ROW_TILE // 2)
    spec = pl.BlockSpec((tm, d), lambda i: (i, 0))
    return pl.pallas_call(
        _out_kernel, grid=(r // tm,),
        in_specs=[spec, spec, spec, spec,
                  pl.BlockSpec((None, d, d), lambda i: (layer, 0, 0)),
                  _layer_spec(norm_g3, (layer + 1) % norm_g3.shape[0])],
        out_specs=[spec, spec],
        out_shape=[jax.ShapeDtypeStruct((r, d), F32), jax.ShapeDtypeStruct((r, d), BF16)],
        compiler_params=_cparams("parallel"),
        name="out_proj",
    )(x, ma, mp, ml, w_out_bf, norm_g3)


def _block_diag(w, tile):
    n, c, _ = w.shape
    per = tile // c
    w4 = w.reshape(n // per, per, c, c)
    eye = jnp.eye(per, dtype=w.dtype)
    return (w4[:, :, :, None, :] * eye[None, :, None, :, None]).reshape(n // per, tile, tile)


def _mixers_and_out(x, xn, lw, layer, attn_fn, nseq, t, pos0, pool_hist, conv_hist, h0, nb, k_all, v_all, mid):
    d = x.shape[-1]
    qn, k_all, kb, v_all, vb, u, gz = _proj_calls(xn, lw["w_in"], layer, lw["pmat"], lw["gq"], lw["gk"],
                                                  k_all, v_all, mid)
    m_attn = attn_fn(qn, k_all, kb, v_all, vb, gz)
    u4 = u.reshape(2, nseq, t, d)
    gz4 = gz.reshape(N_BRANCH, nseq, t, d)
    m_pool = _pool_call(u4, pool_hist, gz4, layer, lw["pool_w"], lw["pool_scale"], pos0, nb)
    m_lru, h_new = _lru_call(u4, conv_hist, h0, gz4, layer, lw["conv_w"], lw["conv_b"],
                             lw["wa"], lw["wx"], lw["ba"], lw["bx"], lw["lam"], nb)
    y, xn_next = _out_call(x, m_attn, m_pool.reshape(nseq * t, d), m_lru.reshape(nseq * t, d),
                           lw["w_out"], layer, lw["norm_g"])
    return y, xn_next, k_all, v_all, u4, h_new.reshape(nseq, d)


def kernel(x_prompt, x_sample, cache_k, cache_v, page_table, state_pool, state_conv, state_h, rel_bias, norm_g, w_in, q_norm_g, k_norm_g, lam_q1, lam_k1, lam_q2, lam_k2, subln_g, pool_w, pool_scale, conv_w, conv_b, gate_a_w, gate_a_b, gate_x_w, gate_x_b, lru_lambda, w_out):
    b, s, d = x_prompt.shape
    db, t_new, _ = x_sample.shape
    depth = w_in.shape[0]
    n_phys, page = cache_k.shape[1], cache_k.shape[2]
    n_pages = page_table.shape[1]
    past = n_pages * page
    assert d == N_HEADS * V_DIM and d % GATE_TILE == 0 and GATE_TILE % (d // LRU_BLOCKS) == 0
    assert s >= POOL_BUF and cache_k.shape[3:] == (N_HEADS, V_DIM)

    w_in_bf = w_in.astype(BF16)
    w_out_bf = w_out.astype(BF16)
    pool_w_bf = pool_w.astype(BF16)
    pmat = jnp.kron(jnp.eye(d // HEAD_DIM, dtype=F32), jnp.ones((HEAD_DIM, HEAD_DIM), F32)).astype(BF16)
    reps = d // HEAD_DIM
    cache_k4 = cache_k.reshape(depth, n_phys, page * N_HEADS, V_DIM)
    cache_v4 = cache_v.reshape(depth, n_phys, page * N_HEADS, V_DIM)
    page_flat = page_table.reshape(-1).astype(jnp.int32)

    t_attn = _tile(s, ATTN_TILE)
    bias_tiles = _prompt_bias_tiles(rel_bias, t_attn)
    masks_s, mask_new = _sample_masks(rel_bias, page, t_new)

    pool0 = jnp.zeros((b, POOL_HIST, d), F32)
    conv0 = jnp.zeros((b, CONV_HIST, d), F32)
    h00 = jnp.zeros((b, 1, d), F32)
    nb_s = 16 if db % 16 == 0 else 1

    xp = x_prompt.reshape(b * s, d)
    xs = x_sample.reshape(db * t_new, d)
    n_gate = LRU_BLOCKS * (d // LRU_BLOCKS) // GATE_TILE
    lw = {
        "w_in": w_in_bf, "w_out": w_out_bf, "pmat": pmat, "norm_g": norm_g[:, None, :],
        "gq": jnp.tile(q_norm_g * (HEAD_DIM ** -0.5 * LOG2E), (1, reps))[:, None, :],
        "gk": jnp.tile(k_norm_g, (1, reps))[:, None, :],
        "pool_w": pool_w_bf, "pool_scale": pool_scale[:, None, :],
        "conv_w": conv_w, "conv_b": conv_b[:, None, :],
        "wa": _block_diag(gate_a_w.reshape((-1,) + gate_a_w.shape[2:]), GATE_TILE).astype(BF16)
        .reshape(depth, n_gate, GATE_TILE, GATE_TILE),
        "wx": _block_diag(gate_x_w.reshape((-1,) + gate_x_w.shape[2:]), GATE_TILE).astype(BF16)
        .reshape(depth, n_gate, GATE_TILE, GATE_TILE),
        "ba": gate_a_b.reshape(depth, 1, d), "bx": gate_x_b.reshape(depth, 1, d),
        "lam": lru_lambda[:, None, :],
    }
    lamp = jnp.stack([lam_q1, lam_k1, lam_q2, lam_k2], axis=1)
    sg_col = subln_g[:, :, None]
    sg_row = subln_g[:, None, :]
    pool_hist = jnp.concatenate([jnp.zeros((depth, db, POOL_HIST - POOL_BUF, d), F32), state_pool], axis=2)
    conv_hist = jnp.concatenate([jnp.zeros((depth, db, CONV_HIST - (CONV_WIDTH - 1), d), F32), state_conv],
                                axis=2)

    xnp = _norm_call(xp, lw["norm_g"])
    xns = _norm_call(xs, lw["norm_g"])

    kp = vp = ksm = vsm = None
    outs = [[] for _ in range(6)]
    for l in range(depth):
        lam_init = 0.8 - 0.6 * math.exp(-0.3 * l)

        attn_p = lambda qn, kn, kb, v, vb, gz: _attn_prompt_call(
            qn, kb, vb, gz, bias_tiles, l, lamp, sg_col, lam_init, b, s)
        xp, xnp, kp, vp, up4, hp = _mixers_and_out(
            xp, xnp, lw, l, attn_p, b, s, 0, pool0, conv0, h00, 1, kp, vp, BF16)

        attn_s = lambda qn, kn, kb, v, vb, gz: _attn_sample_call(
            qn, kn, v, gz, cache_k4, cache_v4, page_flat, l, masks_s, mask_new, lamp, sg_row,
            lam_init, db, t_new, n_pages, page)
        xs, xns, ksm, vsm, us4, hs = _mixers_and_out(
            xs, xns, lw, l, attn_s, db, t_new, past, pool_hist[l], conv_hist[l], state_h[l][:, None], nb_s,
            ksm, vsm, F32)

        new = (up4[0, :, -POOL_BUF:],
               jnp.concatenate([state_pool[l], us4[0]], axis=1)[:, -POOL_BUF:],
               up4[1, :, -(CONV_WIDTH - 1):],
               jnp.concatenate([state_conv[l], us4[1]], axis=1)[:, -(CONV_WIDTH - 1):],
               hp, hs)
        for acc, val in zip(outs, new):
            acc.append(val)
    return ((xp.reshape(b, s, d), xs.reshape(db, t_new, d),
             kp.reshape(depth, b, s, N_HEADS, V_DIM), vp.reshape(depth, b, s, N_HEADS, V_DIM),
             ksm.reshape(depth, db, t_new, N_HEADS, V_DIM), vsm.reshape(depth, db, t_new, N_HEADS, V_DIM))
            + tuple(jnp.stack(o) for o in outs))
```

```python
import functools
import math

import numpy as np
import jax
import jax.numpy as jnp
from jax import lax
from jax.experimental import pallas as pl
from jax.experimental.pallas import tpu as pltpu

F32 = jnp.float32
BF16 = jnp.bfloat16

RMS_EPS = 1e-6
NEG_INF = -1e30
LOG2E = 1.4426950408889634

N_HEADS = 8
HEAD_DIM = 64
V_DIM = 2 * HEAD_DIM
N_BUCKETS = 32
MAX_DISTANCE = 128
POOL_WINDOWS = (2, 4, 8, 16)
POOL_BUF = max(POOL_WINDOWS) - 1
POOL_HIST = 16
CONV_WIDTH = 4
CONV_HIST = 8
LRU_BLOCKS = 16
LRU_C = 8.0
N_BRANCH = 3
GATE_TILE = 256

ROW_TILE = 1024
ATTN_TILE = 512
FAR_UNROLL = 4
SEQ_TILE = 256
SCAN_GROUP = 8
SAMPLE_PAGES = 16
VMEM_LIMIT = 56 * 1024 * 1024


def _cparams(*sem):
    return pltpu.CompilerParams(dimension_semantics=sem, vmem_limit_bytes=VMEM_LIMIT)


def _tile(n, pref):
    return pref if n % pref == 0 else n


def _layer_spec(a, layer):
    zeros = (0,) * (a.ndim - 1)
    return pl.BlockSpec((None,) + a.shape[1:], lambda *_: (layer,) + zeros)


def _rms_rows(x, g):
    ms = jnp.mean(x * x, axis=-1, keepdims=True)
    return x * lax.rsqrt(ms + RMS_EPS) * g


def _norm_kernel(x_ref, g_ref, o_ref):
    o_ref[...] = _rms_rows(x_ref[...], g_ref[...]).astype(BF16)


def _norm_call(x, g):
    r, d = x.shape
    tm = _tile(r, ROW_TILE)
    return pl.pallas_call(
        _norm_kernel,
        grid=(r // tm,),
        in_specs=[pl.BlockSpec((tm, d), lambda i: (i, 0)), _layer_spec(g, 0)],
        out_specs=pl.BlockSpec((tm, d), lambda i: (i, 0)),
        out_shape=jax.ShapeDtypeStruct((r, d), BF16),
        compiler_params=_cparams("parallel"),
        name="prenorm",
    )(x, g)


def _head_norm(acc, pmat_ref, g_ref):
    ss = jnp.dot((acc * acc).astype(BF16), pmat_ref[...], preferred_element_type=F32)
    return acc * lax.rsqrt(ss * (1.0 / HEAD_DIM) + RMS_EPS) * g_ref[...]


def _proj_q_kernel(xn_ref, w_ref, pmat_ref, g_ref, q_ref):
    acc = jnp.dot(xn_ref[...], w_ref[...], preferred_element_type=F32)
    q_ref[...] = _head_norm(acc, pmat_ref, g_ref).astype(BF16)


def _proj_k_kernel(xn_ref, w_ref, pmat_ref, g_ref, *rest):
    k_ref, kb_ref = rest[-2:]
    acc = jnp.dot(xn_ref[...], w_ref[...], preferred_element_type=F32)
    kn = _head_norm(acc, pmat_ref, g_ref)
    k_ref[...] = kn
    kb_ref[...] = kn.astype(BF16)


def _proj_v_kernel(xn_ref, w_ref, *rest):
    v_ref, vb_ref = rest[-2:]
    acc = jnp.dot(xn_ref[...], w_ref[...], preferred_element_type=F32)
    v_ref[...] = acc
    vb_ref[...] = acc.astype(BF16)


def _proj_u_kernel(xn_ref, w_ref, u_ref):
    u_ref[...] = jnp.dot(xn_ref[...], w_ref[...], preferred_element_type=F32)


def _proj_gz_kernel(xn_ref, wz_ref, wg_ref, o_ref):
    xn = xn_ref[...]
    z = jnp.dot(xn, wz_ref[...], preferred_element_type=F32)
    g = jnp.dot(xn, wg_ref[...], preferred_element_type=F32)
    o_ref[...] = (jax.nn.sigmoid(g) * (z * jax.nn.sigmoid(z))).astype(o_ref.dtype)


def _proj_calls(xn, w_bf, layer, pmat, gq, gk, k_all, v_all, mid):
    r, d = xn.shape
    depth = w_bf.shape[0]
    tm = _tile(r, ROW_TILE)
    nr = r // tm
    x_spec = pl.BlockSpec((tm, d), lambda i: (i, 0))
    full = lambda a: pl.BlockSpec(a.shape, lambda i: (0,) * a.ndim)
    o_spec = pl.BlockSpec((tm, d), lambda i: (i, 0))
    all_spec = pl.BlockSpec((None, tm, d), lambda i: (layer, i, 0))
    all_shape = jax.ShapeDtypeStruct((depth, r, d), F32)
    any_spec = pl.BlockSpec(memory_space=pl.ANY)

    def w_spec(j):
        return pl.BlockSpec((None, d, d), lambda i: (layer, 0, j))

    qn = pl.pallas_call(
        _proj_q_kernel, grid=(nr,),
        in_specs=[x_spec, w_spec(0), full(pmat), _layer_spec(gq, layer)],
        out_specs=o_spec, out_shape=jax.ShapeDtypeStruct((r, d), BF16),
        compiler_params=_cparams("parallel"), name="proj_q")(xn, w_bf, pmat, gq)
    k_prev = () if k_all is None else (k_all,)
    k_all, kb = pl.pallas_call(
        _proj_k_kernel, grid=(nr,),
        in_specs=[x_spec, w_spec(1), full(pmat), _layer_spec(gk, layer)] + [any_spec] * len(k_prev),
        out_specs=[all_spec, o_spec],
        out_shape=[all_shape, jax.ShapeDtypeStruct((r, d), BF16)],
        input_output_aliases={4: 0} if k_prev else {},
        compiler_params=_cparams("parallel"), name="proj_k")(xn, w_bf, pmat, gk, *k_prev)
    v_prev = () if v_all is None else (v_all,)
    v_all, vb = pl.pallas_call(
        _proj_v_kernel, grid=(nr,),
        in_specs=[x_spec, w_spec(2)] + [any_spec] * len(v_prev),
        out_specs=[all_spec, o_spec],
        out_shape=[all_shape, jax.ShapeDtypeStruct((r, d), BF16)],
        input_output_aliases={2: 0} if v_prev else {},
        compiler_params=_cparams("parallel"), name="proj_v")(xn, w_bf, *v_prev)
    u = pl.pallas_call(
        _proj_u_kernel, grid=(2, nr),
        in_specs=[pl.BlockSpec((tm, d), lambda j, i: (i, 0)),
                  pl.BlockSpec((None, d, d), lambda j, i: (layer, 0, 3 + j))],
        out_specs=pl.BlockSpec((None, tm, d), lambda j, i: (j, i, 0)),
        out_shape=jax.ShapeDtypeStruct((2, r, d), F32),
        compiler_params=_cparams("parallel", "parallel"), name="proj_u")(xn, w_bf)
    gz = pl.pallas_call(
        _proj_gz_kernel, grid=(N_BRANCH, nr),
        in_specs=[pl.BlockSpec((tm, d), lambda j, i: (i, 0)),
                  pl.BlockSpec((None, d, d), lambda j, i: (layer, 0, 5 + j)),
                  pl.BlockSpec((None, d, d), lambda j, i: (layer, 0, 8 + j))],
        out_specs=pl.BlockSpec((None, tm, d), lambda j, i: (j, i, 0)),
        out_shape=jax.ShapeDtypeStruct((N_BRANCH, r, d), mid),
        compiler_params=_cparams("parallel", "parallel"), name="proj_gz")(xn, w_bf, w_bf)
    return qn, k_all, kb, v_all, vb, u, gz


def _lam_value(lamp_ref, lam_init):
    lp = lamp_ref[...]
    s1 = jnp.sum(lp[0:1] * lp[1:2], axis=1, keepdims=True)
    s2 = jnp.sum(lp[2:3] * lp[3:4], axis=1, keepdims=True)
    return jnp.exp(s1) - jnp.exp(s2) + lam_init


def _bias_by_distance(table, n):
    max_exact = N_BUCKETS // 2
    nf = jnp.maximum(n, 1).astype(F32)
    large = max_exact + (jnp.log(nf / max_exact) / math.log(MAX_DISTANCE / max_exact)
                         * (N_BUCKETS - max_exact)).astype(jnp.int32)
    large = jnp.minimum(large, N_BUCKETS - 1)
    bucket = jnp.where(n < max_exact, n, large)[None]
    col = lambda b: table[b].reshape((-1,) + (1,) * n.ndim)
    out = jnp.broadcast_to(col(0), (table.shape[1],) + n.shape)
    for b in range(1, N_BUCKETS):
        out = jnp.where(bucket == b, col(b), out)
    return out


def _far_distance():
    max_exact = N_BUCKETS // 2
    n = np.arange(1, 4 * MAX_DISTANCE, dtype=np.int64)
    large = max_exact + (np.log(n.astype(np.float32) / np.float32(max_exact))
                         / np.float32(math.log(MAX_DISTANCE / max_exact))
                         * (N_BUCKETS - max_exact)).astype(np.int64)
    not_last = np.nonzero((n < max_exact) | (large < N_BUCKETS - 1))[0]
    return int(n[not_last[-1]] + 1) + 2


def _attn_prompt_kernel(q_ref, k_ref, v_ref, bias_ref, gz_ref, lamp_ref, sg_ref, o_ref,
                        vT_sc, qp_sc, s_sc, m_sc, l_sc, acc_sc, *, t, n_blk, lam_init):
    qi = pl.program_id(2)

    @pl.when(qi == 0)
    def _transpose_values():
        def body(c, carry):
            r0 = pl.multiple_of(c * t, t)
            vT_sc[:, pl.ds(r0, t)] = v_ref[pl.ds(r0, t), :].astype(F32).T.astype(BF16)
            return carry
        lax.fori_loop(0, n_blk, body, 0)

    qT = q_ref[...].astype(F32).T
    row = lax.broadcasted_iota(jnp.int32, qT.shape, 0)
    qp_sc[:, :t] = jnp.where(row < HEAD_DIM, qT, 0.0).astype(BF16)
    qp_sc[:, t:] = jnp.where(row >= HEAD_DIM, qT, 0.0).astype(BF16)
    m_sc[...] = jnp.full(m_sc.shape, NEG_INF, F32)
    l_sc[...] = jnp.zeros(l_sc.shape, F32)
    acc_sc[...] = jnp.zeros(acc_sc.shape, F32)

    def scores(ki, cj, bias_idx):
        r0 = pl.multiple_of(ki * t, t)
        s = jnp.dot(k_ref[pl.ds(r0, t), :], qp_sc[:, cj * t:(cj + 1) * t], preferred_element_type=F32)
        if bias_idx is not None:
            s = s + bias_ref[bias_idx]
        s_sc[cj] = s

    def update(ki, cj):
        r0 = pl.multiple_of(ki * t, t)
        cols = slice(cj * t, (cj + 1) * t)
        s = s_sc[cj]
        m_old = m_sc[:, cols]
        m_new = jnp.maximum(m_old, jnp.max(s, axis=0, keepdims=True))
        alpha = jnp.exp2(m_old - m_new)
        p = jnp.exp2(s - m_new)
        l_sc[:, cols] = alpha * l_sc[:, cols] + jnp.sum(p, axis=0, keepdims=True)
        acc_sc[:, cols] = alpha * acc_sc[:, cols] + jnp.dot(
            vT_sc[:, pl.ds(r0, t)], p.astype(BF16), preferred_element_type=F32)
        m_sc[:, cols] = m_new

    scores(0, 0, None)

    def far_block(ki):
        scores(ki, 1, None)
        update(ki, 0)
        scores(ki + 1, 0, None)
        update(ki, 1)

    def far_group(j, carry):
        for i in range(FAR_UNROLL):
            far_block(FAR_UNROLL * j + i)
        return carry

    n_far = jnp.maximum(qi - 1, 0)
    lax.fori_loop(0, n_far // FAR_UNROLL, far_group, 0)

    def far_rest(ki, carry):
        far_block(ki)
        return carry

    lax.fori_loop(n_far - n_far % FAR_UNROLL, n_far, far_rest, 0)

    @pl.when(qi >= 1)
    def _below_diagonal():
        s_sc[0] = s_sc[0] + bias_ref[1]
        scores(qi - 1, 1, 1)
        update(qi - 1, 0)
        scores(qi, 0, 0)
        update(qi - 1, 1)

    @pl.when(qi == 0)
    def _first_block():
        s_sc[0] = s_sc[0] + bias_ref[0]

    scores(qi, 1, 0)
    update(qi, 0)
    update(qi, 1)

    lam = _lam_value(lamp_ref, lam_init)
    on = acc_sc[...] * (1.0 / l_sc[...])
    o = on[:, :t] - lam * on[:, t:]
    ms = jnp.mean(o * o, axis=0, keepdims=True)
    y = o * lax.rsqrt(ms + RMS_EPS) * sg_ref[...] * (1.0 - lam_init)
    o_ref[...] = (y.T * gz_ref[...]).astype(o_ref.dtype)


def _prompt_bias_tiles(rel_bias, t):
    table = (rel_bias - rel_bias[N_BUCKETS - 1][None, :]) * LOG2E
    kk = jnp.arange(t, dtype=jnp.int32)[:, None]
    qq = jnp.arange(t, dtype=jnp.int32)[None, :]
    dist = jnp.stack([jnp.maximum(qq - kk, 0), t + qq - kk])
    tiles = _bias_by_distance(table, dist)
    return jnp.where((jnp.stack([qq - kk, t + qq - kk]) >= 0)[None], tiles, NEG_INF)


def _attn_prompt_call(qn, kb, vb, gz, bias_tiles, layer, lamp, sg_col, lam_init, b, s):
    d = qn.shape[-1]
    t = _tile(s, ATTN_TILE)
    assert t + 1 >= _far_distance(), "key blocks two or more below the diagonal must sit in the last bias bucket"
    q3, k3, v3 = (a.reshape(b, s, d) for a in (qn, kb, vb))
    gz4 = gz.reshape(N_BRANCH, b, s, d)
    out = pl.pallas_call(
        functools.partial(_attn_prompt_kernel, t=t, n_blk=s // t, lam_init=lam_init),
        grid=(b, N_HEADS, s // t),
        in_specs=[
            pl.BlockSpec((None, t, V_DIM), lambda bi, h, qi: (bi, qi, h)),
            pl.BlockSpec((None, s, V_DIM), lambda bi, h, qi: (bi, 0, h)),
            pl.BlockSpec((None, s, V_DIM), lambda bi, h, qi: (bi, 0, h)),
            pl.BlockSpec((None, 2, t, t), lambda bi, h, qi: (h, 0, 0, 0)),
            pl.BlockSpec((None, None, t, V_DIM), lambda bi, h, qi: (0, bi, qi, h)),
            _layer_spec(lamp, layer),
            _layer_spec(sg_col, layer),
        ],
        out_specs=pl.BlockSpec((None, t, V_DIM), lambda bi, h, qi: (bi, qi, h)),
        out_shape=jax.ShapeDtypeStruct((b, s, d), gz.dtype),
        scratch_shapes=[pltpu.VMEM((V_DIM, s), BF16), pltpu.VMEM((V_DIM, 2 * t), BF16),
                        pltpu.VMEM((2, t, t), F32),
                        pltpu.VMEM((1, 2 * t), F32), pltpu.VMEM((1, 2 * t), F32),
                        pltpu.VMEM((V_DIM, 2 * t), F32)],
        compiler_params=_cparams("parallel", "parallel", "arbitrary"),
        name="attn_prompt",
    )(q3, k3, v3, bias_tiles, gz4, lamp, sg_col)
    return out.reshape(b * s, d)


def _attn_sample_kernel(pt_ref, qbdT_ref, *refs, pp, n_steps, t_new, lam_init):
    kc = refs[:pp]
    vc = refs[pp:2 * pp]
    (kn_ref, vn_ref, mask_ref, maskn_ref, gz_ref, lamp_ref, sg_ref, o_ref, m_sc, l_sc, acc_sc) = refs[2 * pp:]
    p = pl.program_id(1)
    grp = 2 * t_new
    over_rows = (((0,), (0,)), ((), ()))

    @pl.when(p == 0)
    def _init():
        m_sc[...] = jnp.full(m_sc.shape, NEG_INF, F32)
        l_sc[...] = jnp.zeros(l_sc.shape, F32)
        acc_sc[...] = jnp.zeros(acc_sc.shape, F32)

    def attend(k_rows, v_rows, masks):
        s = [jnp.dot(k.astype(BF16), qbdT_ref[...], preferred_element_type=F32) + mk
             for k, mk in zip(k_rows, masks)]
        m_old = m_sc[...]
        m_new = m_old
        for sj in s:
            m_new = jnp.maximum(m_new, jnp.max(sj, axis=0, keepdims=True))
        alpha = jnp.exp2(m_old - m_new)
        l_new = alpha * l_sc[...]
        acc = alpha * acc_sc[...]
        for sj, v in zip(s, v_rows):
            pr = jnp.exp2(sj - m_new)
            l_new = l_new + jnp.sum(pr, axis=0, keepdims=True)
            acc = acc + lax.dot_general(v.astype(BF16), pr.astype(BF16), over_rows,
                                        preferred_element_type=F32)
        l_sc[...] = l_new
        acc_sc[...] = acc
        m_sc[...] = m_new

    last_kind = jnp.where(p == n_steps - 1, 1, 0)
    attend([r[...] for r in kc], [r[...] for r in vc],
           [mask_ref[0]] * (pp - 1) + [mask_ref[last_kind]])

    @pl.when(p == n_steps - 1)
    def _new_tokens():
        attend([kn_ref[...]], [vn_ref[...]], [maskn_ref[...]])
        lam = _lam_value(lamp_ref, lam_init)
        on = (acc_sc[...] * (1.0 / l_sc[...])).T
        for h in range(N_HEADS):
            r1 = h * grp
            o = on[r1:r1 + t_new, :] - lam * on[r1 + t_new:r1 + grp, :]
            ms = jnp.mean(o * o, axis=1, keepdims=True)
            y = o * lax.rsqrt(ms + RMS_EPS) * sg_ref[...] * (1.0 - lam_init)
            lanes = slice(h * V_DIM, (h + 1) * V_DIM)
            o_ref[:, lanes] = y * gz_ref[:, lanes]


def _sample_masks(rel_bias, page, t_new):
    cols = N_HEADS * 2 * t_new
    table = (rel_bias - rel_bias[N_BUCKETS - 1][None, :]) * LOG2E
    col = jnp.arange(cols, dtype=jnp.int32)[None, :]
    col_head, col_t = col // (2 * t_new), col % t_new

    def build(n_tok, first_distance):
        row = jnp.arange(n_tok * N_HEADS, dtype=jnp.int32)[:, None]
        tok, head = row // N_HEADS, row % N_HEADS
        dist = first_distance + col_t - tok
        per_head = _bias_by_distance(table, jnp.maximum(dist, 0))
        vals = per_head[0]
        for h in range(1, N_HEADS):
            vals = jnp.where(head == h, per_head[h], vals)
        return jnp.where((head == col_head) & (dist >= 0), vals, NEG_INF), head == col_head

    last, same_head = build(page, page)
    plain = jnp.where(same_head, 0.0, NEG_INF).astype(F32)
    new, _ = build(t_new, 0)
    return jnp.stack([plain, last]), new


def _attn_sample_call(qn, kn, v, gz, cache_k4, cache_v4, page_flat, layer, masks, mask_new, lamp, sg_row,
                      lam_init, db, t_new, n_pages, page):
    d = qn.shape[-1]
    cols = 2 * N_HEADS * t_new
    pp = SAMPLE_PAGES if n_pages % SAMPLE_PAGES == 0 else 1
    n_steps = n_pages // pp
    assert page + 1 >= _far_distance(), "every cached page but the last must sit in the last bias bucket"
    q5 = jnp.transpose(qn.reshape(db, t_new, N_HEADS, 2, HEAD_DIM), (0, 3, 4, 2, 1))
    eye = jnp.eye(2, dtype=BF16)
    qbdT = (q5[:, :, :, :, None, :] * eye[None, :, None, None, :, None]).reshape(db, V_DIM, cols)
    kn3 = kn[layer].reshape(db, t_new * N_HEADS, V_DIM)
    v3 = v[layer].reshape(db, t_new * N_HEADS, V_DIM)
    gz4 = gz.reshape(N_BRANCH, db, t_new, d)

    def page_spec(j):
        return pl.BlockSpec((None, None, page * N_HEADS, V_DIM),
                            lambda b, p, pt, j=j: (layer, pt[b * n_pages + p * pp + j], 0, 0))

    tok_spec = pl.BlockSpec((None, t_new * N_HEADS, V_DIM), lambda b, p, pt: (b, 0, 0))
    grid_spec = pltpu.PrefetchScalarGridSpec(
        num_scalar_prefetch=1,
        grid=(db, n_steps),
        in_specs=[pl.BlockSpec((None, V_DIM, cols), lambda b, p, pt: (b, 0, 0))]
        + [page_spec(j) for j in range(pp)] + [page_spec(j) for j in range(pp)]
        + [tok_spec, tok_spec,
           pl.BlockSpec(masks.shape, lambda b, p, pt: (0, 0, 0)),
           pl.BlockSpec(mask_new.shape, lambda b, p, pt: (0, 0)),
           pl.BlockSpec((None, None, t_new, d), lambda b, p, pt: (0, b, 0, 0)),
           _layer_spec(lamp, layer), _layer_spec(sg_row, layer)],
        out_specs=pl.BlockSpec((None, t_new, d), lambda b, p, pt: (b, 0, 0)),
        scratch_shapes=[pltpu.VMEM((1, cols), F32), pltpu.VMEM((1, cols), F32),
                        pltpu.VMEM((V_DIM, cols), F32)],
    )
    out = pl.pallas_call(
        functools.partial(_attn_sample_kernel, pp=pp, n_steps=n_steps, t_new=t_new, lam_init=lam_init),
        grid_spec=grid_spec,
        out_shape=jax.ShapeDtypeStruct((db, t_new, d), F32),
        compiler_params=_cparams("parallel", "arbitrary"),
        name="attn_sample",
    )(page_flat, qbdT, *([cache_k4] * pp), *([cache_v4] * pp), kn3, v3, masks, mask_new, gz4, lamp, sg_row)
    return out.reshape(db * t_new, d)


def _pool_kernel(u_ref, hist_ref, gz_ref, w_ref, scale_ref, o_ref, ext_sc, *, tm, pos0, carry):
    c = pl.program_id(1)
    nb = u_ref.shape[0]
    grp = u_ref.shape[2] // len(POOL_WINDOWS)

    @pl.when(c == 0)
    def _hist():
        ext_sc[:, 0:POOL_HIST, :] = hist_ref[...]

    ext_sc[:, POOL_HIST:POOL_HIST + tm, :] = u_ref[...]
    pos = pos0 + c * tm + lax.broadcasted_iota(jnp.int32, (nb, tm, grp), 1)
    for gi, win in enumerate(POOL_WINDOWS):
        lanes = slice(gi * grp, (gi + 1) * grp)
        tot = ext_sc[:, POOL_HIST:POOL_HIST + tm, lanes]
        for j in range(1, win):
            tot = tot + ext_sc[:, POOL_HIST - j:POOL_HIST - j + tm, lanes]
        cnt = jnp.minimum(pos + 1, win).astype(F32)
        pooled = tot / cnt - u_ref[:, :, lanes]
        mixed = jnp.dot(pooled.reshape(nb * tm, grp).astype(BF16), w_ref[gi],
                        preferred_element_type=F32).reshape(nb, tm, grp)
        o_ref[:, :, lanes] = (mixed * scale_ref[:, lanes] * gz_ref[:, :, lanes]).astype(o_ref.dtype)
    if carry:
        ext_sc[:, 0:POOL_HIST, :] = ext_sc[:, tm:tm + POOL_HIST, :]


def _pool_call(u4, hist, gz4, layer, pool_w_bf, pool_scale, pos0, nb):
    _, n, t, d = u4.shape
    tm = _tile(t, SEQ_TILE)
    nc = t // tm
    grp = d // len(POOL_WINDOWS)
    blk = lambda i, c: (i, c, 0)
    return pl.pallas_call(
        functools.partial(_pool_kernel, tm=tm, pos0=pos0, carry=nc > 1),
        grid=(n // nb, nc),
        in_specs=[pl.BlockSpec((None, nb, tm, d), lambda i, c: (0, i, c, 0)),
                  pl.BlockSpec((nb, POOL_HIST, d), lambda i, c: (i, 0, 0)),
                  pl.BlockSpec((None, nb, tm, d), lambda i, c: (1, i, c, 0)),
                  _layer_spec(pool_w_bf, layer), _layer_spec(pool_scale, layer)],
        out_specs=pl.BlockSpec((nb, tm, d), blk),
        out_shape=jax.ShapeDtypeStruct((n, t, d), gz4.dtype),
        scratch_shapes=[pltpu.VMEM((nb, tm + POOL_HIST, d), F32)],
        compiler_params=_cparams("parallel", "arbitrary"),
        name="pool_mix",
    )(u4, hist, gz4, pool_w_bf, pool_scale)


def _lru_kernel(u_ref, hist_ref, h0_ref, gz_ref, cw_ref, cb_ref, wa_ref, wx_ref, ba_ref, bx_ref,
                lam_ref, o_ref, hn_ref, ext_sc, h_sc, *, tm, carry):
    c = pl.program_id(1)
    nb, _, d = u_ref.shape

    @pl.when(c == 0)
    def _state():
        ext_sc[:, 0:CONV_HIST, :] = hist_ref[...]
        h_sc[...] = h0_ref[...]

    ext_sc[:, CONV_HIST:CONV_HIST + tm, :] = u_ref[...]
    xc = cb_ref[...] + ext_sc[:, CONV_HIST - (CONV_WIDTH - 1):CONV_HIST - (CONV_WIDTH - 1) + tm, :] * cw_ref[0:1, :]
    for j in range(1, CONV_WIDTH):
        lo = CONV_HIST - (CONV_WIDTH - 1) + j
        xc = xc + ext_sc[:, lo:lo + tm, :] * cw_ref[j:j + 1, :]
    xb = xc.reshape(nb * tm, d).astype(BF16)
    nt = d // GATE_TILE
    ra = jnp.concatenate([jnp.dot(xb[:, j * GATE_TILE:(j + 1) * GATE_TILE], wa_ref[j],
                                  preferred_element_type=F32) for j in range(nt)], axis=1)
    rx = jnp.concatenate([jnp.dot(xb[:, j * GATE_TILE:(j + 1) * GATE_TILE], wx_ref[j],
                                  preferred_element_type=F32) for j in range(nt)], axis=1)
    r = jax.nn.sigmoid(ra + ba_ref[...]).reshape(nb, tm, d)
    i = jax.nn.sigmoid(rx + bx_ref[...]).reshape(nb, tm, d)
    log_a = (-LRU_C) * r * jax.nn.softplus(-lam_ref[...])
    a = jnp.exp(log_a)
    bt = jnp.sqrt(-jnp.tanh(log_a) * (a * a + 1.0)) * (i * xc)
    row = lax.broadcasted_iota(jnp.int32, (nb, tm, d), 1) % SCAN_GROUP
    sft = 1
    while sft < SCAN_GROUP:
        keep = row >= sft
        a_prev = jnp.where(keep, pltpu.roll(a, sft, axis=1), 1.0)
        b_prev = jnp.where(keep, pltpu.roll(bt, sft, axis=1), 0.0)
        bt = a * b_prev + bt
        a = a * a_prev
        sft *= 2
    h_prev = h_sc[...]
    groups = []
    for g in range(tm // SCAN_GROUP):
        rows = slice(g * SCAN_GROUP, (g + 1) * SCAN_GROUP)
        hg = bt[:, rows, :] + a[:, rows, :] * h_prev
        groups.append(hg)
        h_prev = hg[:, SCAN_GROUP - 1:SCAN_GROUP, :]
    h = jnp.concatenate(groups, axis=1)
    h_last = h_prev
    h_sc[...] = h_last
    hn_ref[...] = h_last
    o_ref[...] = (h * gz_ref[...]).astype(o_ref.dtype)
    if carry:
        ext_sc[:, 0:CONV_HIST, :] = ext_sc[:, tm:tm + CONV_HIST, :]


def _lru_call(u4, hist, h0, gz4, layer, conv_w, conv_b, wa, wx, ba, bx, lam, nb):
    _, n, t, d = u4.shape
    tm = _tile(t, SEQ_TILE)
    nc = t // tm
    blk = lambda i, c: (i, c, 0)
    return pl.pallas_call(
        functools.partial(_lru_kernel, tm=tm, carry=nc > 1),
        grid=(n // nb, nc),
        in_specs=[pl.BlockSpec((None, nb, tm, d), lambda i, c: (1, i, c, 0)),
                  pl.BlockSpec((nb, CONV_HIST, d), lambda i, c: (i, 0, 0)),
                  pl.BlockSpec((nb, 1, d), lambda i, c: (i, 0, 0)),
                  pl.BlockSpec((None, nb, tm, d), lambda i, c: (2, i, c, 0)),
                  ] + [_layer_spec(a, layer) for a in (conv_w, conv_b, wa, wx, ba, bx, lam)],
        out_specs=[pl.BlockSpec((nb, tm, d), blk), pl.BlockSpec((nb, 1, d), lambda i, c: (i, 0, 0))],
        out_shape=[jax.ShapeDtypeStruct((n, t, d), gz4.dtype), jax.ShapeDtypeStruct((n, 1, d), F32)],
        scratch_shapes=[pltpu.VMEM((nb, tm + CONV_HIST, d), F32), pltpu.VMEM((nb, 1, d), F32)],
        compiler_params=_cparams("parallel", "arbitrary"),
        name="conv_rglru",
    )(u4, hist, h0, gz4, conv_w, conv_b, wa, wx, ba, bx, lam)


def _out_kernel(x_ref, ma_ref, mp_ref, ml_ref, w_ref, g_ref, y_ref, xn_ref):
    m = (ma_ref[...].astype(F32) + mp_ref[...].astype(F32) + ml_ref[...].astype(F32)).astype(BF16)
    y = x_ref[...] + jnp.dot(m, w_ref[...], preferred_element_type=F32)
    y_ref[...] = y
    xn_ref[...] = _rms_rows(y, g_ref[...]).astype(BF16)


def _out_call(x, ma, mp, ml, w_out_bf, layer, norm_g3):
    r, d = x.shape
    tm = _tile(r, ROW_TILE // 2)
    spec = pl.BlockSpec((tm, d), lambda i: (i, 0))
    return pl.pallas_call(
        _out_kernel, grid=(r // tm,),
        in_specs=[spec, spec, spec, spec,
                  pl.BlockSpec((None, d, d), lambda i: (layer, 0, 0)),
                  _layer_spec(norm_g3, (layer + 1) % norm_g3.shape[0])],
        out_specs=[spec, spec],
        out_shape=[jax.ShapeDtypeStruct((r, d), F32), jax.ShapeDtypeStruct((r, d), BF16)],
        compiler_params=_cparams("parallel"),
        name="out_proj",
    )(x, ma, mp, ml, w_out_bf, norm_g3)


def _block_diag(w, tile):
    n, c, _ = w.shape
    per = tile // c
    w4 = w.reshape(n // per, per, c, c)
    eye = jnp.eye(per, dtype=w.dtype)
    return (w4[:, :, :, None, :] * eye[None, :, None, :, None]).reshape(n // per, tile, tile)


def _mixers_and_out(x, xn, lw, layer, attn_fn, nseq, t, pos0, pool_hist, conv_hist, h0, nb, k_all, v_all, mid):
    d = x.shape[-1]
    qn, k_all, kb, v_all, vb, u, gz = _proj_calls(xn, lw["w_in"], layer, lw["pmat"], lw["gq"], lw["gk"],
                                                  k_all, v_all, mid)
    m_attn = attn_fn(qn, k_all, kb, v_all, vb, gz)
    u4 = u.reshape(2, nseq, t, d)
    gz4 = gz.reshape(N_BRANCH, nseq, t, d)
    m_pool = _pool_call(u4, pool_hist, gz4, layer, lw["pool_w"], lw["pool_scale"], pos0, nb)
    m_lru, h_new = _lru_call(u4, conv_hist, h0, gz4, layer, lw["conv_w"], lw["conv_b"],
                             lw["wa"], lw["wx"], lw["ba"], lw["bx"], lw["lam"], nb)
    y, xn_next = _out_call(x, m_attn, m_pool.reshape(nseq * t, d), m_lru.reshape(nseq * t, d),
                           lw["w_out"], layer, lw["norm_g"])
    return y, xn_next, k_all, v_all, u4, h_new.reshape(nseq, d)


def kernel(x_prompt, x_sample, cache_k, cache_v, page_table, state_pool, state_conv, state_h, rel_bias, norm_g, w_in, q_norm_g, k_norm_g, lam_q1, lam_k1, lam_q2, lam_k2, subln_g, pool_w, pool_scale, conv_w, conv_b, gate_a_w, gate_a_b, gate_x_w, gate_x_b, lru_lambda, w_out):
    b, s, d = x_prompt.shape
    db, t_new, _ = x_sample.shape
    depth = w_in.shape[0]
    n_phys, page = cache_k.shape[1], cache_k.shape[2]
    n_pages = page_table.shape[1]
    past = n_pages * page
    assert d == N_HEADS * V_DIM and d % GATE_TILE == 0 and GATE_TILE % (d // LRU_BLOCKS) == 0
    assert s >= POOL_BUF and cache_k.shape[3:] == (N_HEADS, V_DIM)

    w_in_bf = w_in.astype(BF16)
    w_out_bf = w_out.astype(BF16)
    pool_w_bf = pool_w.astype(BF16)
    pmat = jnp.kron(jnp.eye(d // HEAD_DIM, dtype=F32), jnp.ones((HEAD_DIM, HEAD_DIM), F32)).astype(BF16)
    reps = d // HEAD_DIM
    cache_k4 = cache_k.reshape(depth, n_phys, page * N_HEADS, V_DIM)
    cache_v4 = cache_v.reshape(depth, n_phys, page * N_HEADS, V_DIM)
    page_flat = page_table.reshape(-1).astype(jnp.int32)

    t_attn = _tile(s, ATTN_TILE)
    bias_tiles = _prompt_bias_tiles(rel_bias, t_attn)
    masks_s, mask_new = _sample_masks(rel_bias, page, t_new)

    pool0 = jnp.zeros((b, POOL_HIST, d), F32)
    conv0 = jnp.zeros((b, CONV_HIST, d), F32)
    h00 = jnp.zeros((b, 1, d), F32)
    nb_s = 16 if db % 16 == 0 else 1

    xp = x_prompt.reshape(b * s, d)
    xs = x_sample.reshape(db * t_new, d)
    n_gate = LRU_BLOCKS * (d // LRU_BLOCKS) // GATE_TILE
    lw = {
        "w_in": w_in_bf, "w_out": w_out_bf, "pmat": pmat, "norm_g": norm_g[:, None, :],
        "gq": jnp.tile(q_norm_g * (HEAD_DIM ** -0.5 * LOG2E), (1, reps))[:, None, :],
        "gk": jnp.tile(k_norm_g, (1, reps))[:, None, :],
        "pool_w": pool_w_bf, "pool_scale": pool_scale[:, None, :],
        "conv_w": conv_w, "conv_b": conv_b[:, None, :],
        "wa": _block_diag(gate_a_w.reshape((-1,) + gate_a_w.shape[2:]), GATE_TILE).astype(BF16)
        .reshape(depth, n_gate, GATE_TILE, GATE_TILE),
        "wx": _block_diag(gate_x_w.reshape((-1,) + gate_x_w.shape[2:]), GATE_TILE).astype(BF16)
        .reshape(depth, n_gate, GATE_TILE, GATE_TILE),
        "ba": gate_a_b.reshape(depth, 1, d), "bx": gate_x_b.reshape(depth, 1, d),
        "lam": lru_lambda[:, None, :],
    }
    lamp = jnp.stack([lam_q1, lam_k1, lam_q2, lam_k2], axis=1)
    sg_col = subln_g[:, :, None]
    sg_row = subln_g[:, None, :]
    pool_hist = jnp.concatenate([jnp.zeros((depth, db, POOL_HIST - POOL_BUF, d), F32), state_pool], axis=2)
    conv_hist = jnp.concatenate([jnp.zeros((depth, db, CONV_HIST - (CONV_WIDTH - 1), d), F32), state_conv],
                                axis=2)

    xnp = _norm_call(xp, lw["norm_g"])
    xns = _norm_call(xs, lw["norm_g"])

    kp = vp = ksm = vsm = None
    outs = [[] for _ in range(6)]
    for l in range(depth):
        lam_init = 0.8 - 0.6 * math.exp(-0.3 * l)

        attn_p = lambda qn, kn, kb, v, vb, gz: _attn_prompt_call(
            qn, kb, vb, gz, bias_tiles, l, lamp, sg_col, lam_init, b, s)
        xp, xnp, kp, vp, up4, hp = _mixers_and_out(
            xp, xnp, lw, l, attn_p, b, s, 0, pool0, conv0, h00, 1, kp, vp, BF16)

        attn_s = lambda qn, kn, kb, v, vb, gz: _attn_sample_call(
            qn, kn, v, gz, cache_k4, cache_v4, page_flat, l, masks_s, mask_new, lamp, sg_row,
            lam_init, db, t_new, n_pages, page)
        xs, xns, ksm, vsm, us4, hs = _mixers_and_out(
            xs, xns, lw, l, attn_s, db, t_new, past, pool_hist[l], conv_hist[l], state_h[l][:, None], nb_s,
            ksm, vsm, F32)

        new = (up4[0, :, -POOL_BUF:],
               jnp.concatenate([state_pool[l], us4[0]], axis=1)[:, -POOL_BUF:],
               up4[1, :, -(CONV_WIDTH - 1):],
               jnp.concatenate([state_conv[l], us4[1]], axis=1)[:, -(CONV_WIDTH - 1):],
               hp, hs)
        for acc, val in zip(outs, new):
            acc.append(val)
    return ((xp.reshape(b, s, d), xs.reshape(db, t_new, d),
             kp.reshape(depth, b, s, N_HEADS, V_DIM), vp.reshape(depth, b, s, N_HEADS, V_DIM),
             ksm.reshape(depth, db, t_new, N_HEADS, V_DIM), vsm.reshape(depth, db, t_new, N_HEADS, V_DIM))
            + tuple(jnp.stack(o) for o in outs))
```
